```python
import math
import jax
import jax.numpy as jnp
from jax import lax
import numpy as np

D_MODEL = 1024
BATCH = 8
SEQ = 4096
DEPTH = 4

MEM_LEN = 256
SB_HEADS = 8
SB_HEAD_DIM = 64
SB_WIDTH = SB_HEADS * SB_HEAD_DIM
SB_BLOCK = 128
GDN_HEADS = 4
GDN_HEAD_DIM = 128
GDN_WIDTH = GDN_HEADS * GDN_HEAD_DIM
GDN_CHUNK = 64
CONV_WIDTH = 4
MIX_WIDTH = SB_WIDTH + GDN_WIDTH
OFF_SB = 3 * SB_WIDTH
OFF_GQKV = OFF_SB + 3 * GDN_WIDTH
OFF_GZ = OFF_GQKV + GDN_WIDTH
OFF_GA = OFF_GZ + GDN_HEADS
N_IN = OFF_GA + GDN_HEADS
XATTN_HEADS = 4
XATTN_HEAD_DIM = D_MODEL // XATTN_HEADS
D_FF = 2816
RMS_EPS = 1e-6
L2_EPS = 1e-6

kernel_name = "hymba_sb_gdn_macaron_memory_trunk"

F32 = jnp.float32


def rms_norm(x, gain, eps=RMS_EPS):
    xf = x.astype(F32)
    y = xf * lax.rsqrt(jnp.mean(xf * xf, axis=-1, keepdims=True) + eps)
    return (y * gain.astype(F32)).astype(x.dtype)


def l2_normalize(x, eps=L2_EPS):
    xf = x.astype(F32)
    return xf * lax.rsqrt(jnp.sum(xf * xf, axis=-1, keepdims=True) + eps)


def swiglu_ffn(x, w_in, w_out):
    gate, up = jnp.split(x @ w_in, 2, axis=-1)
    return (jax.nn.silu(gate) * up) @ w_out


def stick_breaking_attention(q, k, v):
    B, H, S, Dh = q.shape
    scale = Dh ** -0.5
    outs = []
    for blk in range(S // SB_BLOCK):
        q0 = blk * SB_BLOCK
        q1 = q0 + SB_BLOCK
        qb = q[:, :, q0:q1]
        kb = k[:, :, :q1]
        vb = v[:, :, :q1]
        z = jnp.einsum('bhtd,bhsd->bhts', qb, kb, preferred_element_type=F32) * scale
        t_idx = q0 + jnp.arange(SB_BLOCK)[:, None]
        s_idx = jnp.arange(q1)[None, :]
        causal = s_idx < t_idx
        log_beta = jax.nn.log_sigmoid(z)
        log_1m_beta = jnp.where(causal, log_beta - z, 0.0)
        between = lax.cumsum(log_1m_beta, axis=3, reverse=True) - log_1m_beta
        a = jnp.where(causal, jnp.exp(log_beta + between), 0.0)
        outs.append(jnp.einsum('bhts,bhsd->bhtd', a.astype(v.dtype), vb))
    return jnp.concatenate(outs, axis=2)


def causal_depthwise_conv(x, w):
    K, C = w.shape
    return lax.conv_general_dilated(
        x, w[:, None, :].astype(x.dtype), window_strides=(1,), padding=[(K - 1, 0)],
        dimension_numbers=('NWC', 'WIO', 'NWC'), feature_group_count=C)


def gated_delta_rule(q, k, v, g, beta):
    B, H, S, Dk = q.shape
    Dv = v.shape[-1]
    C = GDN_CHUNK
    N = S // C
    q = q * (Dk ** -0.5)

    def chunks(t):
        return t.reshape(B, H, N, C, *t.shape[3:])

    q, k, v, g, beta = chunks(q), chunks(k), chunks(v), chunks(g), chunks(beta)
    g = jnp.cumsum(g, axis=-1)
    tril_incl = jnp.tril(jnp.ones((C, C), dtype=bool))
    strict = jnp.tril(jnp.ones((C, C), dtype=bool), -1)
    diff = g[..., :, None] - g[..., None, :]
    decay = jnp.where(tril_incl, jnp.exp(jnp.where(tril_incl, diff, 0.0)), 0.0)

    kbeta = k * beta[..., None]
    lower = jnp.where(strict, jnp.einsum('bhnck,bhnek->bhnce', kbeta, k) * decay, 0.0)
    eye = jnp.eye(C, dtype=F32)
    rhs = jnp.concatenate([v * beta[..., None], kbeta * jnp.exp(g)[..., None]], axis=-1)
    sol = lax.linalg.triangular_solve(lower + eye, rhs, left_side=True, lower=True)
    u = sol[..., :Dv]
    w = sol[..., Dv:]

    qk = jnp.where(tril_incl, jnp.einsum('bhnck,bhnek->bhnce', q, k) * decay, 0.0)
    q_dec = q * jnp.exp(g)[..., None]
    g_last = g[..., -1]
    k_to_end = k * jnp.exp(g_last[..., None] - g)[..., None]

    def step(state, xs):
        q_i, qk_i, u_i, w_i, gl_i, kend_i = xs
        v_new = u_i - jnp.einsum('bhck,bhkv->bhcv', w_i, state)
        o_i = (jnp.einsum('bhck,bhkv->bhcv', q_i, state)
               + jnp.einsum('bhce,bhev->bhcv', qk_i, v_new))
        state = (state * jnp.exp(gl_i)[..., None, None]
                 + jnp.einsum('bhck,bhcv->bhkv', kend_i, v_new))
        return state, o_i

    xs = tuple(jnp.moveaxis(t, 2, 0) for t in (q_dec, qk, u, w, g_last, k_to_end))
    state0 = jnp.zeros((B, H, Dk, Dv), F32)
    _, o = lax.scan(step, state0, xs)
    return jnp.moveaxis(o, 0, 2).reshape(B, H, S, Dv)


def hybrid_mixer(xn, w_in, conv_w, a_log, dt_bias, sb_out_norm, gdn_out_norm, w_out):
    B, S, _ = xn.shape
    proj = xn @ w_in
    sb_qkv, gdn_qkv, gdn_z, gdn_a, gdn_b = jnp.split(
        proj, [OFF_SB, OFF_GQKV, OFF_GZ, OFF_GA], axis=-1)

    def to_heads(t, h, d):
        return t.reshape(B, S, h, d).transpose(0, 2, 1, 3)

    sq, sk, sv = (to_heads(t, SB_HEADS, SB_HEAD_DIM) for t in jnp.split(sb_qkv, 3, axis=-1))
    sb = stick_breaking_attention(sq, sk, sv)
    sb = rms_norm(sb, sb_out_norm.reshape(SB_HEADS, 1, SB_HEAD_DIM))
    sb = sb.transpose(0, 2, 1, 3).reshape(B, S, SB_WIDTH)

    gdn_qkv = jax.nn.silu(causal_depthwise_conv(gdn_qkv, conv_w))
    gq, gk, gv = (to_heads(t, GDN_HEADS, GDN_HEAD_DIM) for t in jnp.split(gdn_qkv, 3, axis=-1))
    gq, gk = l2_normalize(gq), l2_normalize(gk)
    beta = jax.nn.sigmoid(gdn_b.astype(F32)).transpose(0, 2, 1)
    g = (-jnp.exp(a_log.astype(F32))
         * jax.nn.softplus(gdn_a.astype(F32) + dt_bias.astype(F32))).transpose(0, 2, 1)
    go = gated_delta_rule(gq, gk, gv.astype(F32), g, beta)
    go = go.transpose(0, 2, 1, 3)
    go = rms_norm(go, gdn_out_norm) * jax.nn.silu(
        gdn_z.reshape(B, S, GDN_HEADS, GDN_HEAD_DIM).astype(F32))
    go = go.reshape(B, S, GDN_WIDTH).astype(xn.dtype)

    return jnp.concatenate([sb, go], axis=-1) @ w_out


def memory_cross_attention(xn, memn, w_q, w_kv, w_o):
    B, S, D = xn.shape
    M = memn.shape[1]
    q = (xn @ w_q).reshape(B, S, XATTN_HEADS, XATTN_HEAD_DIM)
    k, v = jnp.split(memn @ w_kv, 2, axis=-1)
    k = k.reshape(B, M, XATTN_HEADS, XATTN_HEAD_DIM)
    v = v.reshape(B, M, XATTN_HEADS, XATTN_HEAD_DIM)
    s = jnp.einsum('bshd,bmhd->bhsm', q, k, preferred_element_type=F32) * (XATTN_HEAD_DIM ** -0.5)
    p = jax.nn.softmax(s, axis=-1).astype(v.dtype)
    o = jnp.einsum('bhsm,bmhd->bshd', p, v).reshape(B, S, D)
    return o @ w_o


def setup_inputs(seed: int = 0) -> dict:
    key = jax.random.key(seed)
    ks = jax.random.split(key, 24)
    L, D = DEPTH, D_MODEL

    def normal(k, shape, scale):
        return jax.random.normal(k, shape, F32) * scale

    def gain(k, shape):
        return 1.0 + 0.02 * jax.random.normal(k, shape, F32)

    dt = jnp.exp(jax.random.uniform(ks[9], (L, GDN_HEADS), F32, math.log(1e-3), math.log(1e-1)))
    return {
        "x": normal(ks[0], (BATCH, SEQ, D), 1.0),
        "mem": normal(ks[1], (BATCH, MEM_LEN, D), 1.0),
        "ffn1_norm": gain(ks[2], (L, D)),
        "ffn1_w_in": normal(ks[3], (L, D, 2 * D_FF), D ** -0.5),
        "ffn1_w_out": normal(ks[4], (L, D_FF, D), D_FF ** -0.5),
        "mix_norm": gain(ks[5], (L, D)),
        "w_in": normal(ks[6], (L, D, N_IN), D ** -0.5),
        "conv_w": normal(ks[7], (L, CONV_WIDTH, 3 * GDN_WIDTH), CONV_WIDTH ** -0.5),
        "a_log": jnp.log(jax.random.uniform(ks[8], (L, GDN_HEADS), F32, 1.0, 16.0)),
        "dt_bias": dt + jnp.log(-jnp.expm1(-dt)),
        "sb_out_norm": gain(ks[10], (L, SB_WIDTH)),
        "gdn_out_norm": gain(ks[11], (L, GDN_HEAD_DIM)),
        "w_out": normal(ks[12], (L, MIX_WIDTH, D), MIX_WIDTH ** -0.5),
        "xattn_norm": gain(ks[13], (L, D)),
        "mem_norm": gain(ks[14], (L, D)),
        "xattn_w_q": normal(ks[15], (L, D, D), D ** -0.5),
        "xattn_w_kv": normal(ks[16], (L, D, 2 * D), D ** -0.5),
        "xattn_w_o": normal(ks[17], (L, D, D), D ** -0.5),
        "ffn2_norm": gain(ks[18], (L, D)),
        "ffn2_w_in": normal(ks[19], (L, D, 2 * D_FF), D ** -0.5),
        "ffn2_w_out": normal(ks[20], (L, D_FF, D), D_FF ** -0.5),
        "final_norm": gain(ks[21], (D,)),
    }


def reference(x, mem, ffn1_norm, ffn1_w_in, ffn1_w_out, mix_norm, w_in, conv_w, a_log, dt_bias,
              sb_out_norm, gdn_out_norm, w_out, xattn_norm, mem_norm, xattn_w_q, xattn_w_kv,
              xattn_w_o, ffn2_norm, ffn2_w_in, ffn2_w_out, final_norm):
    h = x
    for l in range(DEPTH):
        h = h + 0.5 * swiglu_ffn(rms_norm(h, ffn1_norm[l]), ffn1_w_in[l], ffn1_w_out[l])
        h = h + hybrid_mixer(rms_norm(h, mix_norm[l]), w_in[l], conv_w[l], a_log[l], dt_bias[l],
                             sb_out_norm[l], gdn_out_norm[l], w_out[l])
        h = h + memory_cross_attention(rms_norm(h, xattn_norm[l]), rms_norm(mem, mem_norm[l]),
                                       xattn_w_q[l], xattn_w_kv[l], xattn_w_o[l])
        h = h + 0.5 * swiglu_ffn(rms_norm(h, ffn2_norm[l]), ffn2_w_in[l], ffn2_w_out[l])
    return rms_norm(h, final_norm)
```

```python
import functools
import math

import jax
import jax.numpy as jnp
from jax import lax
from jax.experimental import pallas as pl
from jax.experimental.pallas import tpu as pltpu

F32 = jnp.float32
BF16 = jnp.bfloat16

RMS_EPS = 1e-6
L2_EPS = 1e-6

LANES = 128
SB_HEAD_DIM = 64
SB_BLOCK = 128
GDN_HEAD_DIM = 128
GDN_CHUNK = 128
CONV_WIDTH = 4
XATTN_HEAD_DIM = 256
VMEM_LIMIT = 56 * 1024 * 1024

SB_SKIP_LOG = -104.0


def _cparams(*sem):
    return pltpu.CompilerParams(dimension_semantics=sem, vmem_limit_bytes=VMEM_LIMIT)


def _rms(x, gain):
    ms = jnp.mean(x * x, axis=-1, keepdims=True)
    return x * lax.rsqrt(ms + RMS_EPS) * gain


def _dot(a, b):
    return jnp.dot(a, b, preferred_element_type=F32)


def _dot_nt(a, b):
    return lax.dot_general(a, b, (((1,), (1,)), ((), ())), preferred_element_type=F32)


def _dot_tn(a, b):
    return lax.dot_general(a, b, (((0,), (0,)), ((), ())), preferred_element_type=F32)


def _split2(x):
    hi = x.astype(BF16)
    lo = (x - hi.astype(F32)).astype(BF16)
    return hi, lo


def _split3(x):
    hi = x.astype(BF16)
    r = x - hi.astype(F32)
    mid = r.astype(BF16)
    lo = (r - mid.astype(F32)).astype(BF16)
    return hi, mid, lo


def _mm3(a, b):
    ah, al = _split2(a)
    bh, bl = _split2(b)
    return _dot(ah, bh) + (_dot(al, bh) + _dot(ah, bl))


def _softplus(x):
    return jnp.maximum(x, 0.0) + jnp.log(1.0 + jnp.exp(-jnp.abs(x)))


def _silu(x):
    return x * jax.nn.sigmoid(x)


def _ffn_kernel(x_ref, gain_ref, wg_ref, wu_ref, wo_ref, o_ref, xn_ref, acc_ref, *, nf):
    j = pl.program_id(1)

    @pl.when(j == 0)
    def _():
        xn_ref[...] = _rms(x_ref[...], gain_ref[...]).astype(BF16)

    xn = xn_ref[...]
    g = _dot(xn, wg_ref[...])
    u = _dot(xn, wu_ref[...])
    part = _dot((_silu(g) * u).astype(BF16), wo_ref[...])

    @pl.when(j == 0)
    def _():
        acc_ref[...] = part

    @pl.when(j > 0)
    def _():
        acc_ref[...] += part

    @pl.when(j == nf - 1)
    def _():
        o_ref[...] = x_ref[...] + 0.5 * acc_ref[...]


def _ffn(x, gain, w_in, w_out, *, tm=512, nf=2):
    T, D = x.shape
    dff = w_out.shape[0]
    tf = dff // nf
    return pl.pallas_call(
        functools.partial(_ffn_kernel, nf=nf),
        out_shape=jax.ShapeDtypeStruct((T, D), F32),
        grid=(T // tm, nf),
        in_specs=[
            pl.BlockSpec((tm, D), lambda i, j: (i, 0)),
            pl.BlockSpec((1, D), lambda i, j: (0, 0)),
            pl.BlockSpec((D, tf), lambda i, j: (0, j)),
            pl.BlockSpec((D, tf), lambda i, j: (0, j + nf)),
            pl.BlockSpec((tf, D), lambda i, j: (j, 0)),
        ],
        out_specs=pl.BlockSpec((tm, D), lambda i, j: (i, 0)),
        scratch_shapes=[pltpu.VMEM((tm, D), BF16), pltpu.VMEM((tm, D), F32)],
        compiler_params=_cparams("parallel", "arbitrary"),
        name="ffn",
    )(x, gain, w_in, w_in, w_out)


def _inproj_kernel(x_ref, gain_ref, w_ref, q_ref, k_ref, v_ref, gx_ref, z_ref, ab_ref, *, sbw, gw):
    xn = _rms(x_ref[...], gain_ref[...]).astype(BF16)
    o = 0
    q_ref[...] = (_dot(xn, w_ref[:, o:o + sbw]) * (SB_HEAD_DIM ** -0.5)).astype(BF16)
    o += sbw
    k_ref[...] = _dot(xn, w_ref[:, o:o + sbw]).astype(BF16)
    o += sbw
    v_ref[...] = _dot(xn, w_ref[:, o:o + sbw]).astype(BF16)
    o += sbw
    gx_ref[...] = _dot(xn, w_ref[:, o:o + 3 * gw])
    o += 3 * gw
    z_ref[...] = _dot(xn, w_ref[:, o:o + gw])
    o += gw
    ab_ref[...] = _dot(xn, w_ref[:, o:o + LANES])


def _inproj(x, gain, w, *, sbw, gw, tm=512):
    T, D = x.shape
    n = w.shape[1]
    row = lambda i: (i, 0)
    return pl.pallas_call(
        functools.partial(_inproj_kernel, sbw=sbw, gw=gw),
        out_shape=[
            jax.ShapeDtypeStruct((T, sbw), BF16),
            jax.ShapeDtypeStruct((T, sbw), BF16),
            jax.ShapeDtypeStruct((T, sbw), BF16),
            jax.ShapeDtypeStruct((T, 3 * gw), F32),
            jax.ShapeDtypeStruct((T, gw), F32),
            jax.ShapeDtypeStruct((T, LANES), F32),
        ],
        grid=(T // tm,),
        in_specs=[
            pl.BlockSpec((tm, D), row),
            pl.BlockSpec((1, D), lambda i: (0, 0)),
            pl.BlockSpec((D, n), lambda i: (0, 0)),
        ],
        out_specs=[
            pl.BlockSpec((tm, sbw), row),
            pl.BlockSpec((tm, sbw), row),
            pl.BlockSpec((tm, sbw), row),
            pl.BlockSpec((tm, 3 * gw), row),
            pl.BlockSpec((tm, gw), row),
            pl.BlockSpec((tm, LANES), row),
        ],
        compiler_params=_cparams("parallel"),
        name="inproj",
    )(x, gain, w)


def _sb_kernel(q_ref, k_ref, v_ref, gain_ref, o_ref, carry_ref, acc_ref, *, npairs):
    blk = SB_BLOCK
    qi = pl.program_id(1)

    lane = lax.broadcasted_iota(jnp.int32, (blk, LANES), 1)
    lo_half = lane < SB_HEAD_DIM
    row = lax.broadcasted_iota(jnp.int32, (blk, blk), 0)
    col = lax.broadcasted_iota(jnp.int32, (blk, blk), 1)
    causal = col < row
    m2 = jnp.concatenate([jnp.where(row > col, 1.0, 0.0), jnp.ones((blk, blk), F32)], axis=1).astype(BF16)

    def visit(kb, diagonal):
        ks = pl.multiple_of(kb * blk, blk)
        for p in range(npairs):
            ls = slice(p * LANES, (p + 1) * LANES)
            qp = q_ref[:, ls]
            kp = k_ref[pl.ds(ks, blk), ls]
            vp = v_ref[pl.ds(ks, blk), ls]
            contrib = None
            for e in range(2):
                h = 2 * p + e
                sel = lo_half if e == 0 else jnp.logical_not(lo_half)
                z = _dot_nt(jnp.where(sel, qp, jnp.zeros_like(qp)), kp)
                sp = jnp.log(1.0 + jnp.exp(-jnp.abs(z)))
                log_beta = jnp.minimum(z, 0.0) - sp
                log_1m = log_beta - z
                if diagonal:
                    log_1m = jnp.where(causal, log_1m, 0.0)
                hi, lo = _split2(log_1m)
                cs = _dot(hi, m2) + _dot(lo, m2)
                if diagonal:
                    between = cs[:, :blk]
                    carry_ref[h] = cs[:, blk:]
                else:
                    carry = carry_ref[h]
                    between = cs[:, :blk] + carry
                    carry_ref[h] = carry + cs[:, blk:]
                a = jnp.exp(log_beta + between)
                if diagonal:
                    a = jnp.where(causal, a, 0.0)
                c = _dot(a.astype(BF16), jnp.where(sel, vp, jnp.zeros_like(vp)))
                contrib = c if contrib is None else contrib + c
            if diagonal:
                acc_ref[:, ls] = contrib
            else:
                acc_ref[:, ls] += contrib

    def live():
        m = carry_ref[0]
        for h in range(1, 2 * npairs):
            m = jnp.maximum(m, carry_ref[h])
        return (jnp.max(m) > SB_SKIP_LOG).astype(jnp.int32)

    visit(qi, True)

    def cond(c):
        kb, go = c
        return jnp.logical_and(kb >= 0, go > 0)

    def body(c):
        kb, _ = c
        visit(kb, False)
        return kb - 1, live()

    lax.while_loop(cond, body, (qi - 1, live()))

    r = lax.broadcasted_iota(jnp.int32, (LANES, LANES), 0) // SB_HEAD_DIM
    c = lax.broadcasted_iota(jnp.int32, (LANES, LANES), 1) // SB_HEAD_DIM
    ones_blk = jnp.where(r == c, 1.0, 0.0).astype(BF16)
    for p in range(npairs):
        ls = slice(p * LANES, (p + 1) * LANES)
        o = acc_ref[:, ls]
        hi, lo = _split2(o * o)
        ms = (_dot(hi, ones_blk) + _dot(lo, ones_blk)) * (1.0 / SB_HEAD_DIM)
        o_ref[:, ls] = (o * lax.rsqrt(ms + RMS_EPS) * gain_ref[:, ls]).astype(BF16)


def _sb_attention(q, k, v, gain, *, batch):
    T, W = q.shape
    S = T // batch
    nq = S // SB_BLOCK
    npairs = W // LANES
    return pl.pallas_call(
        functools.partial(_sb_kernel, npairs=npairs),
        out_shape=jax.ShapeDtypeStruct((T, W), BF16),
        grid=(batch, nq),
        in_specs=[
            pl.BlockSpec((SB_BLOCK, W), lambda b, i: (b * nq + i, 0)),
            pl.BlockSpec((S, W), lambda b, i: (b, 0)),
            pl.BlockSpec((S, W), lambda b, i: (b, 0)),
            pl.BlockSpec((1, W), lambda b, i: (0, 0)),
        ],
        out_specs=pl.BlockSpec((SB_BLOCK, W), lambda b, i: (b * nq + i, 0)),
        scratch_shapes=[
            pltpu.VMEM((2 * npairs, SB_BLOCK, SB_BLOCK), F32),
            pltpu.VMEM((SB_BLOCK, W), F32),
        ],
        compiler_params=_cparams("parallel", "arbitrary"),
        name="sb_attention",
    )(q, k, v, gain)


def _gdn_kernel(x_ref, z_ref, ab_ref, cw_ref, alog_ref, dtb_ref, gnorm_ref, o_ref,
                halo_ref, state_ref, *, nheads):
    C = GDN_CHUNK
    Dh = GDN_HEAD_DIM
    gw = nheads * Dh
    ci = pl.program_id(1)

    @pl.when(ci == 0)
    def _():
        halo_ref[...] = jnp.zeros_like(halo_ref)
        state_ref[...] = jnp.zeros_like(state_ref)

    x = x_ref[...]
    xx = jnp.concatenate([halo_ref[...], x], axis=0)
    y = x * cw_ref[CONV_WIDTH - 1:CONV_WIDTH, :]
    for back in range(1, CONV_WIDTH):
        y = y + xx[8 - back:8 - back + C, :] * cw_ref[CONV_WIDTH - 1 - back:CONV_WIDTH - back, :]
    halo_ref[...] = x[C - 8:, :]
    y = _silu(y)

    ab = ab_ref[...]
    g_raw = -jnp.exp(alog_ref[...]) * _softplus(ab + dtb_ref[...])
    beta_all = jax.nn.sigmoid(ab)

    row = lax.broadcasted_iota(jnp.int32, (C, C), 0)
    col = lax.broadcasted_iota(jnp.int32, (C, C), 1)
    tril = col <= row
    strict = col < row
    eye = jnp.where(row == col, 1.0, 0.0)
    tril_b = jnp.where(tril, 1.0, 0.0).astype(BF16)

    g1, g2, g3 = _split3(g_raw)
    gc = _dot(tril_b, g1) + (_dot(tril_b, g2) + _dot(tril_b, g3))
    gct = gc.T

    for h in range(nheads):
        q = y[:, h * Dh:(h + 1) * Dh]
        k = y[:, gw + h * Dh:gw + (h + 1) * Dh]
        v = y[:, 2 * gw + h * Dh:2 * gw + (h + 1) * Dh]
        q = q * lax.rsqrt(jnp.sum(q * q, axis=-1, keepdims=True) + L2_EPS) * (Dh ** -0.5)
        k = k * lax.rsqrt(jnp.sum(k * k, axis=-1, keepdims=True) + L2_EPS)
        gcol = gc[:, h:h + 1]
        grow = gct[h:h + 1, :]
        beta = beta_all[:, nheads + h:nheads + h + 1]
        glast = gc[C - 1:C, h:h + 1]

        decay = jnp.where(tril, jnp.exp(jnp.where(tril, gcol - grow, 0.0)), 0.0)
        kb = k * beta
        kb16 = kb.astype(BF16)
        k16 = k.astype(BF16)
        a = jnp.where(strict, -(_dot_nt(kb16, k16) * decay), 0.0)
        qk = jnp.where(tril, _dot_nt(q.astype(BF16), k16) * decay, 0.0)

        tinv = eye + a
        pw = a
        for _ in range(int(math.log2(C)) - 1):
            pw = _mm3(pw, pw)
            tinv = tinv + _mm3(tinv, pw)

        eg = jnp.exp(gcol)
        u = _mm3(tinv, v * beta)
        w = _mm3(tinv, kb * eg)

        st = state_ref[h]
        st16 = st.astype(BF16)
        v_new = u - _dot(w.astype(BF16), st16)
        vn16 = v_new.astype(BF16)
        o = _dot((q * eg).astype(BF16), st16) + _dot(qk.astype(BF16), vn16)
        k_end = (k * jnp.exp(glast - gcol)).astype(BF16)
        state_ref[h] = st * jnp.exp(glast) + _dot_tn(k_end, vn16)

        zz = z_ref[:, h * Dh:(h + 1) * Dh]
        o_ref[:, h * Dh:(h + 1) * Dh] = (_rms(o, gnorm_ref[...]) * _silu(zz)).astype(BF16)


def _gdn(gx, z, ab, conv_w, alog, dtb, gnorm, *, batch, nheads):
    T, gw3 = gx.shape
    gw = gw3 // 3
    S = T // batch
    nc = S // GDN_CHUNK
    C = GDN_CHUNK
    row = lambda b, c: (b * nc + c, 0)
    fixed = lambda b, c: (0, 0)
    return pl.pallas_call(
        functools.partial(_gdn_kernel, nheads=nheads),
        out_shape=jax.ShapeDtypeStruct((T, gw), BF16),
        grid=(batch, nc),
        in_specs=[
            pl.BlockSpec((C, gw3), row),
            pl.BlockSpec((C, gw), row),
            pl.BlockSpec((C, LANES), row),
            pl.BlockSpec((8, gw3), fixed),
            pl.BlockSpec((1, LANES), fixed),
            pl.BlockSpec((1, LANES), fixed),
            pl.BlockSpec((1, GDN_HEAD_DIM), fixed),
        ],
        out_specs=pl.BlockSpec((C, gw), row),
        scratch_shapes=[
            pltpu.VMEM((8, gw3), F32),
            pltpu.VMEM((nheads, GDN_HEAD_DIM, GDN_HEAD_DIM), F32),
        ],
        compiler_params=_cparams("parallel", "arbitrary"),
        name="gdn",
    )(gx, z, ab, conv_w, alog, dtb, gnorm)


def _outproj_kernel(h_ref, sb_ref, go_ref, w_ref, o_ref, *, sbw):
    o_ref[...] = h_ref[...] + (_dot(sb_ref[...], w_ref[:sbw, :]) + _dot(go_ref[...], w_ref[sbw:, :]))


def _outproj(h, sb, go, w, *, tm=512):
    T, D = h.shape
    sbw = sb.shape[1]
    gw = go.shape[1]
    row = lambda i: (i, 0)
    return pl.pallas_call(
        functools.partial(_outproj_kernel, sbw=sbw),
        out_shape=jax.ShapeDtypeStruct((T, D), F32),
        grid=(T // tm,),
        in_specs=[
            pl.BlockSpec((tm, D), row),
            pl.BlockSpec((tm, sbw), row),
            pl.BlockSpec((tm, gw), row),
            pl.BlockSpec((sbw + gw, D), lambda i: (0, 0)),
        ],
        out_specs=pl.BlockSpec((tm, D), row),
        compiler_params=_cparams("parallel"),
        name="outproj",
    )(h, sb, go, w)


def _memkv_kernel(m_ref, gain_ref, w_ref, k_ref, v_ref, *, d):
    mn = _rms(m_ref[...], gain_ref[0]).astype(BF16)
    k_ref[0] = _dot(mn, w_ref[0, :, :d]).astype(BF16)
    v_ref[0] = _dot(mn, w_ref[0, :, d:]).astype(BF16)


def _memkv(mem2d, gains, w_kv):
    R, D = mem2d.shape
    L = w_kv.shape[0]
    return pl.pallas_call(
        functools.partial(_memkv_kernel, d=D),
        out_shape=[jax.ShapeDtypeStruct((L, R, D), BF16), jax.ShapeDtypeStruct((L, R, D), BF16)],
        grid=(L,),
        in_specs=[
            pl.BlockSpec((R, D), lambda l: (0, 0)),
            pl.BlockSpec((1, 1, D), lambda l: (l, 0, 0)),
            pl.BlockSpec((1, D, 2 * D), lambda l: (l, 0, 0)),
        ],
        out_specs=[
            pl.BlockSpec((1, R, D), lambda l: (l, 0, 0)),
            pl.BlockSpec((1, R, D), lambda l: (l, 0, 0)),
        ],
        compiler_params=_cparams("parallel"),
        name="memkv",
    )(mem2d, gains, w_kv)


def _xattn_kernel(h_ref, gain_ref, wq_ref, k_ref, v_ref, wo_ref, o_ref, *, nheads):
    Dh = XATTN_HEAD_DIM
    x = h_ref[...]
    xn = _rms(x, gain_ref[...]).astype(BF16)
    outs = []
    for hd in range(nheads):
        ls = slice(hd * Dh, (hd + 1) * Dh)
        qh = _dot(xn, wq_ref[:, ls]).astype(BF16)
        s = _dot_nt(qh, k_ref[:, ls])
        p = jnp.exp(s - jnp.max(s, axis=-1, keepdims=True))
        denom = jnp.sum(p, axis=-1, keepdims=True)
        outs.append((_dot(p.astype(BF16), v_ref[:, ls]) / denom).astype(BF16))
    o = jnp.concatenate(outs, axis=-1)
    o_ref[...] = x + _dot(o, wo_ref[...])


def _xattn(h, gain, wq, kmem, vmem, wo, *, batch, tm=512):
    T, D = h.shape
    S = T // batch
    per = S // tm
    M = kmem.shape[0] // batch
    row = lambda i: (i, 0)
    fixed = lambda i: (0, 0)
    return pl.pallas_call(
        functools.partial(_xattn_kernel, nheads=D // XATTN_HEAD_DIM),
        out_shape=jax.ShapeDtypeStruct((T, D), F32),
        grid=(T // tm,),
        in_specs=[
            pl.BlockSpec((tm, D), row),
            pl.BlockSpec((1, D), fixed),
            pl.BlockSpec((D, D), fixed),
            pl.BlockSpec((M, D), lambda i: (i // per, 0)),
            pl.BlockSpec((M, D), lambda i: (i // per, 0)),
            pl.BlockSpec((D, D), fixed),
        ],
        out_specs=pl.BlockSpec((tm, D), row),
        compiler_params=_cparams("parallel"),
        name="xattn",
    )(h, gain, wq, kmem, vmem, wo)


def _final_norm_kernel(x_ref, gain_ref, o_ref):
    o_ref[...] = _rms(x_ref[...], gain_ref[...])


def _final_norm(h, gain, *, tm=512):
    T, D = h.shape
    return pl.pallas_call(
        _final_norm_kernel,
        out_shape=jax.ShapeDtypeStruct((T, D), F32),
        grid=(T // tm,),
        in_specs=[pl.BlockSpec((tm, D), lambda i: (i, 0)), pl.BlockSpec((1, D), lambda i: (0, 0))],
        out_specs=pl.BlockSpec((tm, D), lambda i: (i, 0)),
        compiler_params=_cparams("parallel"),
        name="final_norm",
    )(h, gain)


def _lane_vec(v):
    return jnp.zeros((1, LANES), F32).at[0, :v.shape[0]].set(v.astype(F32))


def kernel(x, mem, ffn1_norm, ffn1_w_in, ffn1_w_out, mix_norm, w_in, conv_w, a_log, dt_bias, sb_out_norm, gdn_out_norm, w_out, xattn_norm, mem_norm, xattn_w_q, xattn_w_kv, xattn_w_o, ffn2_norm, ffn2_w_in, ffn2_w_out, final_norm):
    B, S, D = x.shape
    L = ffn1_norm.shape[0]
    nh_gdn = a_log.shape[1]
    gw = nh_gdn * GDN_HEAD_DIM
    sbw = sb_out_norm.shape[1]
    M = mem.shape[1]
    assert S % 512 == 0 and D % LANES == 0 and sbw % LANES == 0
    assert w_in.shape[2] == 3 * sbw + 4 * gw + 2 * nh_gdn and 2 * nh_gdn <= LANES
    T = B * S

    bf = lambda w: w.astype(BF16)
    w_mix = bf(jnp.pad(w_in, ((0, 0), (0, 0), (0, LANES - 2 * nh_gdn))))
    conv_w8 = jnp.pad(conv_w.astype(F32), ((0, 0), (0, 8 - CONV_WIDTH), (0, 0)))
    wq = bf(xattn_w_q * (XATTN_HEAD_DIM ** -0.5))
    kmem, vmem = _memkv(mem.reshape(B * M, D), mem_norm.reshape(L, 1, D), bf(xattn_w_kv))

    h = x.reshape(T, D)
    for l in range(L):
        h = _ffn(h, ffn1_norm[l][None], bf(ffn1_w_in[l]), bf(ffn1_w_out[l]))
        q, k, v, gx, z, ab = _inproj(h, mix_norm[l][None], w_mix[l], sbw=sbw, gw=gw)
        sb = _sb_attention(q, k, v, sb_out_norm[l][None], batch=B)
        go = _gdn(gx, z, ab, conv_w8[l], _lane_vec(a_log[l]), _lane_vec(dt_bias[l]),
                  gdn_out_norm[l][None], batch=B, nheads=nh_gdn)
        h = _outproj(h, sb, go, bf(w_out[l]))
        h = _xattn(h, xattn_norm[l][None], wq[l], kmem[l], vmem[l], bf(xattn_w_o[l]), batch=B)
        h = _ffn(h, ffn2_norm[l][None], bf(ffn2_w_in[l]), bf(ffn2_w_out[l]))
    return _final_norm(h, final_norm[None]).reshape(B, S, D)
```

```python
import functools
import math

import jax
import jax.numpy as jnp
from jax import lax
from jax.experimental import pallas as pl
from jax.experimental.pallas import tpu as pltpu

F32 = jnp.float32
BF16 = jnp.bfloat16

RMS_EPS = 1e-6
L2_EPS = 1e-6

LANES = 128
SB_HEAD_DIM = 64
SB_BLOCK = 128
GDN_HEAD_DIM = 128
GDN_CHUNK = 128
CONV_WIDTH = 4
XATTN_HEAD_DIM = 256
VMEM_LIMIT = 56 * 1024 * 1024

SB_SKIP_LOG = -104.0


def _cparams(*sem):
    return pltpu.CompilerParams(dimension_semantics=sem, vmem_limit_bytes=VMEM_LIMIT)


def _rms(x, gain):
    ms = jnp.mean(x * x, axis=-1, keepdims=True)
    return x * lax.rsqrt(ms + RMS_EPS) * gain


def _dot(a, b):
    return jnp.dot(a, b, preferred_element_type=F32)


def _dot_nt(a, b):
    return lax.dot_general(a, b, (((1,), (1,)), ((), ())), preferred_element_type=F32)


def _split2(x):
    hi = x.astype(BF16)
    lo = (x - hi.astype(F32)).astype(BF16)
    return hi, lo


def _split3(x):
    hi = x.astype(BF16)
    r = x - hi.astype(F32)
    mid = r.astype(BF16)
    lo = (r - mid.astype(F32)).astype(BF16)
    return hi, mid, lo


def _lhs3(hi, lo):
    return jnp.concatenate([hi, lo], axis=1)


def _rhs3(hi, lo):
    top = jnp.concatenate([hi, lo], axis=1)
    bot = jnp.concatenate([hi, jnp.zeros_like(lo)], axis=1)
    return jnp.concatenate([top, bot], axis=0)


def _mm3(lhs3, rhs3):
    r = _dot(lhs3, rhs3)
    n = r.shape[1] // 2
    return r[:, :n] + r[:, n:]


def _softplus(x):
    return jnp.maximum(x, 0.0) + jnp.log(1.0 + jnp.exp(-jnp.abs(x)))


def _silu(x):
    return x * jax.nn.sigmoid(x)


def _ffn_kernel(x_ref, gain_ref, wg_ref, wu_ref, wo_ref, o_ref, *, nsub):
    ts = x_ref.shape[0] // nsub
    rows = [slice(s * ts, (s + 1) * ts) for s in range(nsub)]
    act = [None] * nsub
    for s in range(nsub + 1):
        if s < nsub:
            xn = _rms(x_ref[rows[s], :], gain_ref[...]).astype(BF16)
            g = _dot(xn, wg_ref[...])
            u = _dot(xn, wu_ref[...])
            act[s] = (_silu(g) * u).astype(BF16)
        if s >= 1:
            r = rows[s - 1]
            o_ref[r, :] = x_ref[r, :] + 0.5 * _dot(act[s - 1], wo_ref[...])


def _ffn(x, gain, w_in, w_out, *, tm=1024, nsub=4):
    T, D = x.shape
    dff = w_out.shape[0]
    resident = dict(pipeline_mode=pl.Buffered(1))
    return pl.pallas_call(
        functools.partial(_ffn_kernel, nsub=nsub),
        out_shape=jax.ShapeDtypeStruct((T, D), F32),
        grid=(T // tm,),
        in_specs=[
            pl.BlockSpec((tm, D), lambda i: (i, 0)),
            pl.BlockSpec((1, D), lambda i: (0, 0)),
            pl.BlockSpec((D, dff), lambda i: (0, 0), **resident),
            pl.BlockSpec((D, dff), lambda i: (0, 1), **resident),
            pl.BlockSpec((dff, D), lambda i: (0, 0), **resident),
        ],
        out_specs=pl.BlockSpec((tm, D), lambda i: (i, 0)),
        compiler_params=_cparams("parallel"),
        name="ffn",
    )(x, gain, w_in, w_in, w_out)


def _inproj_kernel(x_ref, gain_ref, w_ref, q_ref, k_ref, v_ref, gx_ref, z_ref, ab_ref, *, sbw, gw):
    xn = _rms(x_ref[...], gain_ref[...]).astype(BF16)
    o = 0
    q_ref[...] = (_dot(xn, w_ref[:, o:o + sbw]) * (SB_HEAD_DIM ** -0.5)).astype(BF16)
    o += sbw
    k_ref[...] = _dot(xn, w_ref[:, o:o + sbw]).astype(BF16)
    o += sbw
    v_ref[...] = _dot(xn, w_ref[:, o:o + sbw]).astype(BF16)
    o += sbw
    gx_ref[...] = _dot(xn, w_ref[:, o:o + 3 * gw])
    o += 3 * gw
    z_ref[...] = _dot(xn, w_ref[:, o:o + gw])
    o += gw
    ab_ref[...] = _dot(xn, w_ref[:, o:o + LANES])


def _inproj(x, gain, w, *, sbw, gw, tm=512):
    T, D = x.shape
    n = w.shape[1]
    row = lambda i: (i, 0)
    return pl.pallas_call(
        functools.partial(_inproj_kernel, sbw=sbw, gw=gw),
        out_shape=[
            jax.ShapeDtypeStruct((T, sbw), BF16),
            jax.ShapeDtypeStruct((T, sbw), BF16),
            jax.ShapeDtypeStruct((T, sbw), BF16),
            jax.ShapeDtypeStruct((T, 3 * gw), F32),
            jax.ShapeDtypeStruct((T, gw), F32),
            jax.ShapeDtypeStruct((T, LANES), F32),
        ],
        grid=(T // tm,),
        in_specs=[
            pl.BlockSpec((tm, D), row),
            pl.BlockSpec((1, D), lambda i: (0, 0)),
            pl.BlockSpec((D, n), lambda i: (0, 0)),
        ],
        out_specs=[
            pl.BlockSpec((tm, sbw), row),
            pl.BlockSpec((tm, sbw), row),
            pl.BlockSpec((tm, sbw), row),
            pl.BlockSpec((tm, 3 * gw), row),
            pl.BlockSpec((tm, gw), row),
            pl.BlockSpec((tm, LANES), row),
        ],
        compiler_params=_cparams("parallel"),
        name="inproj",
    )(x, gain, w)


def _sb_kernel(q_ref, k_ref, v_ref, gain_ref, o_ref, q2_ref, carry_ref, acc_ref, *, npairs):
    blk = SB_BLOCK
    qi = pl.program_id(1)

    lane = lax.broadcasted_iota(jnp.int32, (blk, LANES), 1)
    lo_half = lane < SB_HEAD_DIM
    row = lax.broadcasted_iota(jnp.int32, (blk, blk), 0)
    col = lax.broadcasted_iota(jnp.int32, (blk, blk), 1)
    causal = jnp.concatenate([col < row] * 2, axis=0)
    m2 = jnp.concatenate([jnp.where(row > col, 1.0, 0.0), jnp.ones((blk, blk), F32)], axis=1).astype(BF16)
    m22 = jnp.concatenate([m2, m2], axis=0)

    def halves(x):
        zero = jnp.zeros_like(x)
        return jnp.concatenate([jnp.where(lo_half, x, zero), jnp.where(lo_half, zero, x)], axis=0)

    def visit(kb, diagonal):
        ks = pl.multiple_of(kb * blk, blk)
        P = range(npairs)
        ls = [slice(p * LANES, (p + 1) * LANES) for p in P]
        if diagonal:
            for p in P:
                q2_ref[p] = halves(q_ref[:, ls[p]])
        z = [_dot_nt(q2_ref[p], k_ref[pl.ds(ks, blk), ls[p]]) for p in P]
        log_beta, cs = [], []
        for p in P:
            sp = jnp.log(1.0 + jnp.exp(-jnp.abs(z[p])))
            lb = jnp.minimum(z[p], 0.0) - sp
            log_1m = lb - z[p]
            if diagonal:
                log_1m = jnp.where(causal, log_1m, 0.0)
            log_beta.append(lb)
            cs.append(_dot(_lhs3(*_split2(log_1m)), m22))
        for p in P:
            if diagonal:
                between = cs[p][:, :blk]
                carry_ref[p] = cs[p][:, blk:]
            else:
                carry = carry_ref[p]
                between = cs[p][:, :blk] + carry
                carry_ref[p] = carry + cs[p][:, blk:]
            a = jnp.exp(log_beta[p] + between)
            if diagonal:
                a = jnp.where(causal, a, 0.0)
            a = a.astype(BF16)
            contrib = _dot(jnp.concatenate([a[:blk], a[blk:]], axis=1),
                           halves(v_ref[pl.ds(ks, blk), ls[p]]))
            if diagonal:
                acc_ref[:, ls[p]] = contrib
            else:
                acc_ref[:, ls[p]] += contrib

    def live():
        m = carry_ref[0]
        for p in range(1, npairs):
            m = jnp.maximum(m, carry_ref[p])
        return (jnp.max(m) > SB_SKIP_LOG).astype(jnp.int32)

    visit(qi, True)

    def cond(c):
        kb, go = c
        return jnp.logical_and(kb >= 0, go > 0)

    def body(c):
        kb, _ = c
        visit(kb, False)
        return kb - 1, live()

    lax.while_loop(cond, body, (qi - 1, live()))

    r = lax.broadcasted_iota(jnp.int32, (LANES, LANES), 0) // SB_HEAD_DIM
    c = lax.broadcasted_iota(jnp.int32, (LANES, LANES), 1) // SB_HEAD_DIM
    ones_blk = jnp.where(r == c, 1.0, 0.0).astype(BF16)
    ones2 = jnp.concatenate([ones_blk, ones_blk], axis=0)
    for p in range(npairs):
        ls = slice(p * LANES, (p + 1) * LANES)
        o = acc_ref[:, ls]
        hi, lo = _split2(o * o)
        ms = _dot(_lhs3(hi, lo), ones2) * (1.0 / SB_HEAD_DIM)
        o_ref[:, ls] = (o * lax.rsqrt(ms + RMS_EPS) * gain_ref[:, ls]).astype(BF16)


def _sb_attention(q, k, v, gain, *, batch):
    T, W = q.shape
    S = T // batch
    nq = S // SB_BLOCK
    npairs = W // LANES
    return pl.pallas_call(
        functools.partial(_sb_kernel, npairs=npairs),
        out_shape=jax.ShapeDtypeStruct((T, W), BF16),
        grid=(batch, nq),
        in_specs=[
            pl.BlockSpec((SB_BLOCK, W), lambda b, i: (b * nq + i, 0)),
            pl.BlockSpec((S, W), lambda b, i: (b, 0)),
            pl.BlockSpec((S, W), lambda b, i: (b, 0)),
            pl.BlockSpec((1, W), lambda b, i: (0, 0)),
        ],
        out_specs=pl.BlockSpec((SB_BLOCK, W), lambda b, i: (b * nq + i, 0)),
        scratch_shapes=[
            pltpu.VMEM((npairs, 2 * SB_BLOCK, LANES), BF16),
            pltpu.VMEM((npairs, 2 * SB_BLOCK, SB_BLOCK), F32),
            pltpu.VMEM((SB_BLOCK, W), F32),
        ],
        compiler_params=_cparams("parallel", "arbitrary"),
        name="sb_attention",
    )(q, k, v, gain)


def _gdn_kernel(x_ref, z_ref, ab_ref, cw_ref, alog_ref, dtb_ref, gnorm_ref, o_ref,
                halo_ref, state_ref, *, nheads):
    C = GDN_CHUNK
    Dh = GDN_HEAD_DIM
    gw = nheads * Dh
    ci = pl.program_id(1)

    @pl.when(ci == 0)
    def _():
        halo_ref[...] = jnp.zeros_like(halo_ref)
        state_ref[...] = jnp.zeros_like(state_ref)

    x = x_ref[...]
    xx = jnp.concatenate([halo_ref[...], x], axis=0)
    y = x * cw_ref[CONV_WIDTH - 1:CONV_WIDTH, :]
    for back in range(1, CONV_WIDTH):
        y = y + xx[8 - back:8 - back + C, :] * cw_ref[CONV_WIDTH - 1 - back:CONV_WIDTH - back, :]
    halo_ref[...] = x[C - 8:, :]
    y = _silu(y)

    ab = ab_ref[...]
    g_raw = -jnp.exp(alog_ref[...]) * _softplus(ab + dtb_ref[...])
    beta_all = jax.nn.sigmoid(ab)

    row = lax.broadcasted_iota(jnp.int32, (C, C), 0)
    col = lax.broadcasted_iota(jnp.int32, (C, C), 1)
    tril = col <= row
    strict = col < row
    eye = jnp.where(row == col, 1.0, 0.0)
    tril_b = jnp.where(tril, 1.0, 0.0).astype(BF16)

    g1, g2, g3 = _split3(g_raw)
    gc3 = _dot(tril_b, jnp.concatenate([g1, g2, g3], axis=1))
    gc = gc3[:, :LANES] + (gc3[:, LANES:2 * LANES] + gc3[:, 2 * LANES:])
    gct = gc.T

    H = range(nheads)
    q, k, v, gcol, beta, glast, decay, kb = ([None] * nheads for _ in range(8))
    for h in H:
        qh = y[:, h * Dh:(h + 1) * Dh]
        kh = y[:, gw + h * Dh:gw + (h + 1) * Dh]
        v[h] = y[:, 2 * gw + h * Dh:2 * gw + (h + 1) * Dh]
        q[h] = qh * lax.rsqrt(jnp.sum(qh * qh, axis=-1, keepdims=True) + L2_EPS) * (Dh ** -0.5)
        k[h] = kh * lax.rsqrt(jnp.sum(kh * kh, axis=-1, keepdims=True) + L2_EPS)
        gcol[h] = gc[:, h:h + 1]
        grow = gct[h:h + 1, :]
        beta[h] = beta_all[:, nheads + h:nheads + h + 1]
        glast[h] = gc[C - 1:C, h:h + 1]
        decay[h] = jnp.where(tril, jnp.exp(jnp.where(tril, gcol[h] - grow, 0.0)), 0.0)
        kb[h] = k[h] * beta[h]

    kk_qk = [_dot_nt(jnp.concatenate([kb[h].astype(BF16), q[h].astype(BF16)], axis=0), k[h].astype(BF16))
             for h in H]
    a = [jnp.where(strict, -(kk_qk[h][:C] * decay[h]), 0.0) for h in H]
    qk = [jnp.where(tril, kk_qk[h][C:] * decay[h], 0.0) for h in H]

    tinv = [eye + a[h] for h in H]
    psplit = [_split2(a[h]) for h in H]
    pw = [_mm3(_lhs3(*psplit[h]), _rhs3(*psplit[h])) for h in H]
    levels = int(math.log2(C)) - 1
    for lvl in range(levels):
        psplit = [_split2(pw[h]) for h in H]
        tsplit = [_split2(tinv[h]) for h in H]
        if lvl < levels - 1:
            r = [_mm3(jnp.concatenate([_lhs3(*psplit[h]), _lhs3(*tsplit[h])], axis=0), _rhs3(*psplit[h]))
                 for h in H]
            pw = [r[h][:C] for h in H]
            tinv = [tinv[h] + r[h][C:] for h in H]
        else:
            tinv = [tinv[h] + _mm3(_lhs3(*tsplit[h]), _rhs3(*psplit[h])) for h in H]

    eg = [jnp.exp(gcol[h]) for h in H]
    tsplit = [_split2(tinv[h]) for h in H]
    rsplit = [_split2(jnp.concatenate([v[h] * beta[h], kb[h] * eg[h]], axis=1)) for h in H]
    uw = [_mm3(_lhs3(*tsplit[h]), _rhs3(*rsplit[h])) for h in H]

    st = [state_ref[h] for h in H]
    ws_qs = [_dot(jnp.concatenate([uw[h][:, Dh:].astype(BF16), (q[h] * eg[h]).astype(BF16)], axis=0),
                  st[h].astype(BF16)) for h in H]
    vn16 = [(uw[h][:, :Dh] - ws_qs[h][:C]).astype(BF16) for h in H]
    k_end_t = [(k[h] * jnp.exp(glast[h] - gcol[h])).T.astype(BF16) for h in H]
    ov_sv = [_dot(jnp.concatenate([qk[h].astype(BF16), k_end_t[h]], axis=0), vn16[h]) for h in H]
    for h in H:
        o = ws_qs[h][C:] + ov_sv[h][:C]
        state_ref[h] = st[h] * jnp.exp(glast[h]) + ov_sv[h][C:]
        zz = z_ref[:, h * Dh:(h + 1) * Dh]
        o_ref[:, h * Dh:(h + 1) * Dh] = (_rms(o, gnorm_ref[...]) * _silu(zz)).astype(BF16)


def _gdn(gx, z, ab, conv_w, alog, dtb, gnorm, *, batch, nheads):
    T, gw3 = gx.shape
    gw = gw3 // 3
    S = T // batch
    nc = S // GDN_CHUNK
    C = GDN_CHUNK
    row = lambda b, c: (b * nc + c, 0)
    fixed = lambda b, c: (0, 0)
    return pl.pallas_call(
        functools.partial(_gdn_kernel, nheads=nheads),
        out_shape=jax.ShapeDtypeStruct((T, gw), BF16),
        grid=(batch, nc),
        in_specs=[
            pl.BlockSpec((C, gw3), row),
            pl.BlockSpec((C, gw), row),
            pl.BlockSpec((C, LANES), row),
            pl.BlockSpec((8, gw3), fixed),
            pl.BlockSpec((1, LANES), fixed),
            pl.BlockSpec((1, LANES), fixed),
            pl.BlockSpec((1, GDN_HEAD_DIM), fixed),
        ],
        out_specs=pl.BlockSpec((C, gw), row),
        scratch_shapes=[
            pltpu.VMEM((8, gw3), F32),
            pltpu.VMEM((nheads, GDN_HEAD_DIM, GDN_HEAD_DIM), F32),
        ],
        compiler_params=_cparams("parallel", "arbitrary"),
        name="gdn",
    )(gx, z, ab, conv_w, alog, dtb, gnorm)


def _memkv_kernel(m_ref, gain_ref, w_ref, k_ref, v_ref, *, d):
    mn = _rms(m_ref[...], gain_ref[0]).astype(BF16)
    k_ref[0] = _dot(mn, w_ref[0, :, :d]).astype(BF16)
    v_ref[0] = _dot(mn, w_ref[0, :, d:]).astype(BF16)


def _memkv(mem2d, gains, w_kv):
    R, D = mem2d.shape
    L = w_kv.shape[0]
    return pl.pallas_call(
        functools.partial(_memkv_kernel, d=D),
        out_shape=[jax.ShapeDtypeStruct((L, R, D), BF16), jax.ShapeDtypeStruct((L, R, D), BF16)],
        grid=(L,),
        in_specs=[
            pl.BlockSpec((R, D), lambda l: (0, 0)),
            pl.BlockSpec((1, 1, D), lambda l: (l, 0, 0)),
            pl.BlockSpec((1, D, 2 * D), lambda l: (l, 0, 0)),
        ],
        out_specs=[
            pl.BlockSpec((1, R, D), lambda l: (l, 0, 0)),
            pl.BlockSpec((1, R, D), lambda l: (l, 0, 0)),
        ],
        compiler_params=_cparams("parallel"),
        name="memkv",
    )(mem2d, gains, w_kv)


def _mixout_xattn_kernel(h_ref, sb_ref, go_ref, wmix_ref, gain_ref, wq_ref, k_ref, v_ref, wo_ref, o_ref,
                        *, nheads, nsub, sbw):
    Dh = XATTN_HEAD_DIM
    ts = h_ref.shape[0] // nsub
    R = range(nsub)
    H = range(nheads)
    rows = [slice(s * ts, (s + 1) * ts) for s in R]
    ls = [slice(hd * Dh, (hd + 1) * Dh) for hd in H]
    x = [h_ref[rows[s], :] + (_dot(sb_ref[rows[s], :], wmix_ref[:sbw, :]) + _dot(go_ref[rows[s], :], wmix_ref[sbw:, :]))
         for s in R]
    q = [_dot(_rms(x[s], gain_ref[...]).astype(BF16), wq_ref[...]).astype(BF16) for s in R]
    sc = [[_dot_nt(q[s][:, ls[hd]], k_ref[:, ls[hd]]) for hd in H] for s in R]
    p = [[jnp.exp(sc[s][hd] - jnp.max(sc[s][hd], axis=-1, keepdims=True)) for hd in H] for s in R]
    pv = [[_dot(p[s][hd].astype(BF16), v_ref[:, ls[hd]]) for hd in H] for s in R]
    for s in R:
        o = jnp.concatenate([(pv[s][hd] / jnp.sum(p[s][hd], axis=-1, keepdims=True)).astype(BF16) for hd in H],
                            axis=-1)
        o_ref[rows[s], :] = x[s] + _dot(o, wo_ref[...])


def _mixout_xattn(h, sb, go, wmix, gain, wq, kmem, vmem, wo, *, batch, tm=512, nsub=2):
    T, D = h.shape
    S = T // batch
    per = S // tm
    M = kmem.shape[0] // batch
    sbw = sb.shape[1]
    gw = go.shape[1]
    row = lambda i: (i, 0)
    fixed = lambda i: (0, 0)
    return pl.pallas_call(
        functools.partial(_mixout_xattn_kernel, nheads=D // XATTN_HEAD_DIM, nsub=nsub, sbw=sbw),
        out_shape=jax.ShapeDtypeStruct((T, D), F32),
        grid=(T // tm,),
        in_specs=[
            pl.BlockSpec((tm, D), row),
            pl.BlockSpec((tm, sbw), row),
            pl.BlockSpec((tm, gw), row),
            pl.BlockSpec((sbw + gw, D), fixed),
            pl.BlockSpec((1, D), fixed),
            pl.BlockSpec((D, D), fixed),
            pl.BlockSpec((M, D), lambda i: (i // per, 0)),
            pl.BlockSpec((M, D), lambda i: (i // per, 0)),
            pl.BlockSpec((D, D), fixed),
        ],
        out_specs=pl.BlockSpec((tm, D), row),
        compiler_params=_cparams("parallel"),
        name="mixout_xattn",
    )(h, sb, go, wmix, gain, wq, kmem, vmem, wo)


def _final_norm_kernel(x_ref, gain_ref, o_ref):
    o_ref[...] = _rms(x_ref[...], gain_ref[...])


def _final_norm(h, gain, *, tm=512):
    T, D = h.shape
    return pl.pallas_call(
        _final_norm_kernel,
        out_shape=jax.ShapeDtypeStruct((T, D), F32),
        grid=(T // tm,),
        in_specs=[pl.BlockSpec((tm, D), lambda i: (i, 0)), pl.BlockSpec((1, D), lambda i: (0, 0))],
        out_specs=pl.BlockSpec((tm, D), lambda i: (i, 0)),
        compiler_params=_cparams("parallel"),
        name="final_norm",
    )(h, gain)


def _lane_vec(v):
    return jnp.zeros((1, LANES), F32).at[0, :v.shape[0]].set(v.astype(F32))


def kernel(x, mem, ffn1_norm, ffn1_w_in, ffn1_w_out, mix_norm, w_in, conv_w, a_log, dt_bias, sb_out_norm, gdn_out_norm, w_out, xattn_norm, mem_norm, xattn_w_q, xattn_w_kv, xattn_w_o, ffn2_norm, ffn2_w_in, ffn2_w_out, final_norm):
    B, S, D = x.shape
    L = ffn1_norm.shape[0]
    nh_gdn = a_log.shape[1]
    gw = nh_gdn * GDN_HEAD_DIM
    sbw = sb_out_norm.shape[1]
    M = mem.shape[1]
    assert S % 512 == 0 and (B * S) % 1024 == 0 and D % LANES == 0 and sbw % LANES == 0
    assert w_in.shape[2] == 3 * sbw + 4 * gw + 2 * nh_gdn and 2 * nh_gdn <= LANES
    T = B * S

    bf = lambda w: w.astype(BF16)
    w_mix = bf(jnp.pad(w_in, ((0, 0), (0, 0), (0, LANES - 2 * nh_gdn))))
    conv_w8 = jnp.pad(conv_w.astype(F32), ((0, 0), (0, 8 - CONV_WIDTH), (0, 0)))
    wq = bf(xattn_w_q * (XATTN_HEAD_DIM ** -0.5))
    kmem, vmem = _memkv(mem.reshape(B * M, D), mem_norm.reshape(L, 1, D), bf(xattn_w_kv))

    h = x.reshape(T, D)
    for l in range(L):
        h = _ffn(h, ffn1_norm[l][None], bf(ffn1_w_in[l]), bf(ffn1_w_out[l]))
        q, k, v, gx, z, ab = _inproj(h, mix_norm[l][None], w_mix[l], sbw=sbw, gw=gw)
        sb = _sb_attention(q, k, v, sb_out_norm[l][None], batch=B)
        go = _gdn(gx, z, ab, conv_w8[l], _lane_vec(a_log[l]), _lane_vec(dt_bias[l]),
                  gdn_out_norm[l][None], batch=B, nheads=nh_gdn)
        h = _mixout_xattn(h, sb, go, bf(w_out[l]), xattn_norm[l][None], wq[l], kmem[l], vmem[l],
                          bf(xattn_w_o[l]), batch=B)
        h = _ffn(h, ffn2_norm[l][None], bf(ffn2_w_in[l]), bf(ffn2_w_out[l]))
    return _final_norm(h, final_norm[None]).reshape(B, S, D)
```

```python
import functools
import math

import jax
import jax.numpy as jnp
from jax import lax
from jax.experimental import pallas as pl
from jax.experimental.pallas import tpu as pltpu

F32 = jnp.float32
BF16 = jnp.bfloat16

RMS_EPS = 1e-6
L2_EPS = 1e-6

LANES = 128
SB_HEAD_DIM = 64
SB_BLOCK = 128
GDN_HEAD_DIM = 128
GDN_CHUNK = 128
GDN_CHUNKS_PER_STEP = 2
CONV_WIDTH = 4
XATTN_HEAD_DIM = 256
VMEM_LIMIT = 56 * 1024 * 1024

SB_SKIP_LOG = -87.5


def _cparams(*sem):
    return pltpu.CompilerParams(dimension_semantics=sem, vmem_limit_bytes=VMEM_LIMIT)


def _rms(x, gain):
    ms = jnp.mean(x * x, axis=-1, keepdims=True)
    return x * lax.rsqrt(ms + RMS_EPS) * gain


def _dot(a, b):
    return jnp.dot(a, b, preferred_element_type=F32)


def _dot_nt(a, b):
    return lax.dot_general(a, b, (((1,), (1,)), ((), ())), preferred_element_type=F32)


def _split2(x):
    hi = x.astype(BF16)
    lo = (x - hi.astype(F32)).astype(BF16)
    return hi, lo


def _split3(x):
    hi = x.astype(BF16)
    r = x - hi.astype(F32)
    mid = r.astype(BF16)
    lo = (r - mid.astype(F32)).astype(BF16)
    return hi, mid, lo


def _lhs3(hi, lo):
    return jnp.concatenate([hi, lo], axis=1)


def _rhs3(hi, lo):
    top = jnp.concatenate([hi, lo], axis=1)
    bot = jnp.concatenate([hi, jnp.zeros_like(lo)], axis=1)
    return jnp.concatenate([top, bot], axis=0)


def _mm3(lhs3, rhs3):
    r = _dot(lhs3, rhs3)
    n = r.shape[1] // 2
    return r[:, :n] + r[:, n:]


def _softplus(x):
    return jnp.maximum(x, 0.0) + jnp.log(1.0 + jnp.exp(-jnp.abs(x)))


def _silu(x):
    return x * jax.nn.sigmoid(x)


def _ffn_kernel(x_ref, gain_ref, wg_ref, wu_ref, wo_ref, *rest, nsub):
    o_ref = rest[-1]
    ts = x_ref.shape[0] // nsub
    rows = [slice(s * ts, (s + 1) * ts) for s in range(nsub)]
    act = [None] * nsub
    for s in range(nsub + 1):
        if s < nsub:
            xn = _rms(x_ref[rows[s], :], gain_ref[...]).astype(BF16)
            g = _dot(xn, wg_ref[...])
            u = _dot(xn, wu_ref[...])
            act[s] = (_silu(g) * u).astype(BF16)
        if s >= 1:
            r = rows[s - 1]
            out = x_ref[r, :] + 0.5 * _dot(act[s - 1], wo_ref[...])
            o_ref[r, :] = _rms(out, rest[0][...]) if len(rest) == 2 else out


def _ffn(x, gain, w_in, w_out, out_gain=None, *, tm=1024, nsub=4):
    T, D = x.shape
    dff = w_out.shape[0]
    resident = dict(pipeline_mode=pl.Buffered(1))
    vec = pl.BlockSpec((1, D), lambda i: (0, 0))
    tail = () if out_gain is None else (out_gain,)
    return pl.pallas_call(
        functools.partial(_ffn_kernel, nsub=nsub),
        out_shape=jax.ShapeDtypeStruct((T, D), F32),
        grid=(T // tm,),
        in_specs=[
            pl.BlockSpec((tm, D), lambda i: (i, 0)),
            vec,
            pl.BlockSpec((D, dff), lambda i: (0, 0), **resident),
            pl.BlockSpec((D, dff), lambda i: (0, 1), **resident),
            pl.BlockSpec((dff, D), lambda i: (0, 0), **resident),
        ] + [vec] * len(tail),
        out_specs=pl.BlockSpec((tm, D), lambda i: (i, 0)),
        compiler_params=_cparams("parallel"),
        name="ffn",
    )(x, gain, w_in, w_in, w_out, *tail)


def _inproj_kernel(x_ref, gain_ref, w_ref, cw_ref, q_ref, k_ref, v_ref, gx_ref, z_ref, ab_ref, halo_ref,
                   *, sbw, gw, tiles_per_seq):
    tm = x_ref.shape[0]
    Dh = GDN_HEAD_DIM

    @pl.when(pl.program_id(0) % tiles_per_seq == 0)
    def _():
        halo_ref[:8, :] = jnp.zeros((8, halo_ref.shape[1]), F32)

    xn = _rms(x_ref[...], gain_ref[...]).astype(BF16)
    o = 3 * sbw
    for part in range(3):
        cs = slice(part * gw, (part + 1) * gw)
        graw = _dot(xn, w_ref[:, o + part * gw:o + (part + 1) * gw])
        halo_ref[8:, cs] = graw
        y = graw * cw_ref[CONV_WIDTH - 1:CONV_WIDTH, cs]
        for back in range(1, CONV_WIDTH):
            y = y + halo_ref[8 - back:8 - back + tm, cs] * cw_ref[CONV_WIDTH - 1 - back:CONV_WIDTH - back, cs]
        halo_ref[:8, cs] = graw[tm - 8:, :]
        y = _silu(y)
        if part == 2:
            gx_ref[:, cs] = y
        else:
            for j in range(gw // Dh):
                seg = y[:, j * Dh:(j + 1) * Dh]
                seg = seg * lax.rsqrt(jnp.sum(seg * seg, axis=-1, keepdims=True) + L2_EPS)
                gx_ref[:, part * gw + j * Dh:part * gw + (j + 1) * Dh] = seg * (Dh ** -0.5) if part == 0 else seg
    o += 3 * gw
    z_ref[...] = _dot(xn, w_ref[:, o:o + gw])
    o += gw
    ab_ref[...] = _dot(xn, w_ref[:, o:o + LANES])
    q_ref[...] = (_dot(xn, w_ref[:, :sbw]) * (SB_HEAD_DIM ** -0.5)).astype(BF16)
    k_ref[...] = _dot(xn, w_ref[:, sbw:2 * sbw]).astype(BF16)
    v_ref[...] = _dot(xn, w_ref[:, 2 * sbw:3 * sbw]).astype(BF16)


def _inproj(x, gain, w, conv_w, *, sbw, gw, batch, tm=512):
    T, D = x.shape
    n = w.shape[1]
    row = lambda i: (i, 0)
    return pl.pallas_call(
        functools.partial(_inproj_kernel, sbw=sbw, gw=gw, tiles_per_seq=T // batch // tm),
        out_shape=[
            jax.ShapeDtypeStruct((T, sbw), BF16),
            jax.ShapeDtypeStruct((T, sbw), BF16),
            jax.ShapeDtypeStruct((T, sbw), BF16),
            jax.ShapeDtypeStruct((T, 3 * gw), F32),
            jax.ShapeDtypeStruct((T, gw), F32),
            jax.ShapeDtypeStruct((T, LANES), F32),
        ],
        grid=(T // tm,),
        in_specs=[
            pl.BlockSpec((tm, D), row),
            pl.BlockSpec((1, D), lambda i: (0, 0)),
            pl.BlockSpec((D, n), lambda i: (0, 0)),
            pl.BlockSpec((8, 3 * gw), lambda i: (0, 0)),
        ],
        out_specs=[
            pl.BlockSpec((tm, sbw), row),
            pl.BlockSpec((tm, sbw), row),
            pl.BlockSpec((tm, sbw), row),
            pl.BlockSpec((tm, 3 * gw), row),
            pl.BlockSpec((tm, gw), row),
            pl.BlockSpec((tm, LANES), row),
        ],
        scratch_shapes=[pltpu.VMEM((8 + tm, 3 * gw), F32)],
        compiler_params=_cparams("arbitrary"),
        name="inproj",
    )(x, gain, w, conv_w)


def _sb_kernel(q_ref, k_ref, v_ref, gain_ref, o_ref, q2_ref, carry_ref, acc_ref, *, npairs):
    blk = SB_BLOCK
    qi = pl.program_id(1)

    lane = lax.broadcasted_iota(jnp.int32, (blk, LANES), 1)
    lo_half = lane < SB_HEAD_DIM
    row = lax.broadcasted_iota(jnp.int32, (blk, blk), 0)
    col = lax.broadcasted_iota(jnp.int32, (blk, blk), 1)
    causal = jnp.concatenate([col < row] * 2, axis=0)
    m2 = jnp.concatenate([jnp.where(row > col, 1.0, 0.0), jnp.ones((blk, blk), F32)], axis=1).astype(BF16)

    def halves(x):
        zero = jnp.zeros_like(x)
        return jnp.concatenate([jnp.where(lo_half, x, zero), jnp.where(lo_half, zero, x)], axis=0)

    def visit(kb, diagonal):
        ks = pl.multiple_of(kb * blk, blk)
        P = range(npairs)
        ls = [slice(p * LANES, (p + 1) * LANES) for p in P]
        if diagonal:
            for p in P:
                q2_ref[p] = halves(q_ref[:, ls[p]])
        z = [_dot_nt(q2_ref[p], k_ref[pl.ds(ks, blk), ls[p]]) for p in P]
        log_beta, cs = [], []
        for p in P:
            sp = jnp.log(1.0 + jnp.exp(-jnp.abs(z[p])))
            lb = jnp.minimum(z[p], 0.0) - sp
            log_1m = lb - z[p]
            if diagonal:
                log_1m = jnp.where(causal, log_1m, 0.0)
            log_beta.append(lb)
            cs.append(_dot(log_1m.astype(BF16), m2))
        for p in P:
            if diagonal:
                between = cs[p][:, :blk]
                carry_ref[p] = cs[p][:, blk:]
            else:
                carry = carry_ref[p]
                between = cs[p][:, :blk] + carry
                carry_ref[p] = carry + cs[p][:, blk:]
            a = jnp.exp(log_beta[p] + between)
            if diagonal:
                a = jnp.where(causal, a, 0.0)
            a = a.astype(BF16)
            contrib = _dot(jnp.concatenate([a[:blk], a[blk:]], axis=1),
                           halves(v_ref[pl.ds(ks, blk), ls[p]]))
            if diagonal:
                acc_ref[:, ls[p]] = contrib
            else:
                acc_ref[:, ls[p]] += contrib

    def live():
        m = carry_ref[0]
        for p in range(1, npairs):
            m = jnp.maximum(m, carry_ref[p])
        return (jnp.max(m) > SB_SKIP_LOG).astype(jnp.int32)

    visit(qi, True)

    def cond(c):
        kb, go = c
        return jnp.logical_and(kb >= 0, go > 0)

    def body(c):
        kb, _ = c
        visit(kb, False)
        return kb - 1, live()

    lax.while_loop(cond, body, (qi - 1, live()))

    r = lax.broadcasted_iota(jnp.int32, (LANES, LANES), 0) // SB_HEAD_DIM
    c = lax.broadcasted_iota(jnp.int32, (LANES, LANES), 1) // SB_HEAD_DIM
    ones_blk = jnp.where(r == c, 1.0, 0.0).astype(BF16)
    ones2 = jnp.concatenate([ones_blk, ones_blk], axis=0)
    for p in range(npairs):
        ls = slice(p * LANES, (p + 1) * LANES)
        o = acc_ref[:, ls]
        hi, lo = _split2(o * o)
        ms = _dot(_lhs3(hi, lo), ones2) * (1.0 / SB_HEAD_DIM)
        o_ref[:, ls] = (o * lax.rsqrt(ms + RMS_EPS) * gain_ref[:, ls]).astype(BF16)


def _sb_attention(q, k, v, gain, *, batch):
    T, W = q.shape
    S = T // batch
    nq = S // SB_BLOCK
    npairs = W // LANES
    return pl.pallas_call(
        functools.partial(_sb_kernel, npairs=npairs),
        out_shape=jax.ShapeDtypeStruct((T, W), BF16),
        grid=(batch, nq),
        in_specs=[
            pl.BlockSpec((SB_BLOCK, W), lambda b, i: (b * nq + i, 0)),
            pl.BlockSpec((S, W), lambda b, i: (b, 0)),
            pl.BlockSpec((S, W), lambda b, i: (b, 0)),
            pl.BlockSpec((1, W), lambda b, i: (0, 0)),
        ],
        out_specs=pl.BlockSpec((SB_BLOCK, W), lambda b, i: (b * nq + i, 0)),
        scratch_shapes=[
            pltpu.VMEM((npairs, 2 * SB_BLOCK, LANES), BF16),
            pltpu.VMEM((npairs, 2 * SB_BLOCK, SB_BLOCK), F32),
            pltpu.VMEM((SB_BLOCK, W), F32),
        ],
        compiler_params=_cparams("parallel", "arbitrary"),
        name="sb_attention",
    )(q, k, v, gain)


def _gdn_kernel(x_ref, z_ref, ab_ref, alog_ref, dtb_ref, gnorm_ref, o_ref, state_ref, *, nheads, nchunks):
    C = GDN_CHUNK
    Dh = GDN_HEAD_DIM
    gw = nheads * Dh

    @pl.when(pl.program_id(1) == 0)
    def _():
        state_ref[...] = jnp.zeros_like(state_ref)

    ab = ab_ref[...]
    g_raw = -jnp.exp(alog_ref[...]) * _softplus(ab + dtb_ref[...])
    beta_all = jax.nn.sigmoid(ab)

    row = lax.broadcasted_iota(jnp.int32, (C, C), 0)
    col = lax.broadcasted_iota(jnp.int32, (C, C), 1)
    tril = col <= row
    strict = col < row
    eye = jnp.where(row == col, 1.0, 0.0)
    tril_b = jnp.where(tril, 1.0, 0.0).astype(BF16)

    gc, gct = [], []
    for c in range(nchunks):
        g1, g2, g3 = _split3(g_raw[c * C:(c + 1) * C, :])
        gc3 = _dot(tril_b, jnp.concatenate([g1, g2, g3], axis=1))
        gc.append(gc3[:, :LANES] + (gc3[:, LANES:2 * LANES] + gc3[:, 2 * LANES:]))
        gct.append(gc[c].T)

    units = [(c, h) for c in range(nchunks) for h in range(nheads)]
    U = range(len(units))
    q, k, v, gcol, beta, glast, decay, kb = ([None] * len(units) for _ in range(8))
    for i, (c, h) in enumerate(units):
        rs = slice(c * C, (c + 1) * C)
        q[i] = x_ref[rs, h * Dh:(h + 1) * Dh]
        k[i] = x_ref[rs, gw + h * Dh:gw + (h + 1) * Dh]
        v[i] = x_ref[rs, 2 * gw + h * Dh:2 * gw + (h + 1) * Dh]
        gcol[i] = gc[c][:, h:h + 1]
        grow = gct[c][h:h + 1, :]
        beta[i] = beta_all[rs, nheads + h:nheads + h + 1]
        glast[i] = gc[c][C - 1:C, h:h + 1]
        decay[i] = jnp.where(tril, jnp.exp(jnp.where(tril, gcol[i] - grow, 0.0)), 0.0)
        kb[i] = k[i] * beta[i]

    kk_qk = [_dot_nt(jnp.concatenate([kb[i].astype(BF16), q[i].astype(BF16)], axis=0), k[i].astype(BF16))
             for i in U]
    a = [jnp.where(strict, -(kk_qk[i][:C] * decay[i]), 0.0) for i in U]
    qk = [jnp.where(tril, kk_qk[i][C:] * decay[i], 0.0).astype(BF16) for i in U]

    tinv = [eye + a[i] for i in U]
    psplit = [_split2(a[i]) for i in U]
    pw = [_mm3(_lhs3(*psplit[i]), _rhs3(*psplit[i])) for i in U]
    levels = int(math.log2(C)) - 1
    for lvl in range(levels):
        psplit = [_split2(pw[i]) for i in U]
        tsplit = [_split2(tinv[i]) for i in U]
        if lvl < levels - 1:
            r = [_mm3(jnp.concatenate([_lhs3(*psplit[i]), _lhs3(*tsplit[i])], axis=0), _rhs3(*psplit[i]))
                 for i in U]
            pw = [r[i][:C] for i in U]
            tinv = [tinv[i] + r[i][C:] for i in U]
        else:
            tinv = [tinv[i] + _mm3(_lhs3(*tsplit[i]), _rhs3(*psplit[i])) for i in U]

    eg = [jnp.exp(gcol[i]) for i in U]
    uw = [_dot(tinv[i].astype(BF16), jnp.concatenate([v[i] * beta[i], kb[i] * eg[i]], axis=1).astype(BF16))
          for i in U]
    wq16 = [jnp.concatenate([uw[i][:, Dh:].astype(BF16), (q[i] * eg[i]).astype(BF16)], axis=0) for i in U]
    k_end_t = [(k[i] * jnp.exp(glast[i] - gcol[i])).T.astype(BF16) for i in U]

    st = [state_ref[h] for h in range(nheads)]
    for c in range(nchunks):
        ids = [c * nheads + h for h in range(nheads)]
        ws_qs = [_dot(wq16[i], st[h].astype(BF16)) for h, i in enumerate(ids)]
        vn16 = [(uw[i][:, :Dh] - ws_qs[h][:C]).astype(BF16) for h, i in enumerate(ids)]
        ov_sv = [_dot(jnp.concatenate([qk[i], k_end_t[i]], axis=0), vn16[h])
                 for h, i in enumerate(ids)]
        for h, i in enumerate(ids):
            o = ws_qs[h][C:] + ov_sv[h][:C]
            st[h] = st[h] * jnp.exp(glast[i]) + ov_sv[h][C:]
            zz = z_ref[c * C:(c + 1) * C, h * Dh:(h + 1) * Dh]
            o_ref[c * C:(c + 1) * C, h * Dh:(h + 1) * Dh] = (_rms(o, gnorm_ref[...]) * _silu(zz)).astype(BF16)
    for h in range(nheads):
        state_ref[h] = st[h]


def _gdn(gx, z, ab, alog, dtb, gnorm, *, batch, nheads, nchunks=GDN_CHUNKS_PER_STEP):
    T, gw3 = gx.shape
    gw = gw3 // 3
    S = T // batch
    C = nchunks * GDN_CHUNK
    nc = S // C
    row = lambda b, c: (b * nc + c, 0)
    fixed = lambda b, c: (0, 0)
    return pl.pallas_call(
        functools.partial(_gdn_kernel, nheads=nheads, nchunks=nchunks),
        out_shape=jax.ShapeDtypeStruct((T, gw), BF16),
        grid=(batch, nc),
        in_specs=[
            pl.BlockSpec((C, gw3), row),
            pl.BlockSpec((C, gw), row),
            pl.BlockSpec((C, LANES), row),
            pl.BlockSpec((1, LANES), fixed),
            pl.BlockSpec((1, LANES), fixed),
            pl.BlockSpec((1, GDN_HEAD_DIM), fixed),
        ],
        out_specs=pl.BlockSpec((C, gw), row),
        scratch_shapes=[pltpu.VMEM((nheads, GDN_HEAD_DIM, GDN_HEAD_DIM), F32)],
        compiler_params=_cparams("parallel", "arbitrary"),
        name="gdn",
    )(gx, z, ab, alog, dtb, gnorm)


def _memkv_kernel(m_ref, gain_ref, w_ref, k_ref, v_ref, *, d):
    mn = _rms(m_ref[...], gain_ref[0]).astype(BF16)
    k_ref[0] = _dot(mn, w_ref[0, :, :d]).astype(BF16)
    v_ref[0] = _dot(mn, w_ref[0, :, d:]).astype(BF16)


def _memkv(mem2d, gains, w_kv):
    R, D = mem2d.shape
    L = w_kv.shape[0]
    return pl.pallas_call(
        functools.partial(_memkv_kernel, d=D),
        out_shape=[jax.ShapeDtypeStruct((L, R, D), BF16), jax.ShapeDtypeStruct((L, R, D), BF16)],
        grid=(L,),
        in_specs=[
            pl.BlockSpec((R, D), lambda l: (0, 0)),
            pl.BlockSpec((1, 1, D), lambda l: (l, 0, 0)),
            pl.BlockSpec((1, D, 2 * D), lambda l: (l, 0, 0)),
        ],
        out_specs=[
            pl.BlockSpec((1, R, D), lambda l: (l, 0, 0)),
            pl.BlockSpec((1, R, D), lambda l: (l, 0, 0)),
        ],
        compiler_params=_cparams("parallel"),
        name="memkv",
    )(mem2d, gains, w_kv)


def _mixout_xattn_kernel(h_ref, sb_ref, go_ref, wmix_ref, gain_ref, wq_ref, k_ref, v_ref, wo_ref, o_ref,
                        *, nheads, nsub, sbw):
    Dh = XATTN_HEAD_DIM
    ts = h_ref.shape[0] // nsub
    R = range(nsub)
    H = range(nheads)
    rows = [slice(s * ts, (s + 1) * ts) for s in R]
    ls = [slice(hd * Dh, (hd + 1) * Dh) for hd in H]
    x = [h_ref[rows[s], :] + (_dot(sb_ref[rows[s], :], wmix_ref[:sbw, :]) + _dot(go_ref[rows[s], :], wmix_ref[sbw:, :]))
         for s in R]
    q = [_dot(_rms(x[s], gain_ref[...]).astype(BF16), wq_ref[...]).astype(BF16) for s in R]
    sc = [[_dot_nt(q[s][:, ls[hd]], k_ref[:, ls[hd]]) for hd in H] for s in R]
    p = [[jnp.exp(sc[s][hd] - jnp.max(sc[s][hd], axis=-1, keepdims=True)) for hd in H] for s in R]
    pv = [[_dot(p[s][hd].astype(BF16), v_ref[:, ls[hd]]) for hd in H] for s in R]
    for s in R:
        o = jnp.concatenate([(pv[s][hd] / jnp.sum(p[s][hd], axis=-1, keepdims=True)).astype(BF16) for hd in H],
                            axis=-1)
        o_ref[rows[s], :] = x[s] + _dot(o, wo_ref[...])


def _mixout_xattn(h, sb, go, wmix, gain, wq, kmem, vmem, wo, *, batch, tm=512, nsub=2):
    T, D = h.shape
    S = T // batch
    per = S // tm
    M = kmem.shape[0] // batch
    sbw = sb.shape[1]
    gw = go.shape[1]
    row = lambda i: (i, 0)
    fixed = lambda i: (0, 0)
    return pl.pallas_call(
        functools.partial(_mixout_xattn_kernel, nheads=D // XATTN_HEAD_DIM, nsub=nsub, sbw=sbw),
        out_shape=jax.ShapeDtypeStruct((T, D), F32),
        grid=(T // tm,),
        in_specs=[
            pl.BlockSpec((tm, D), row),
            pl.BlockSpec((tm, sbw), row),
            pl.BlockSpec((tm, gw), row),
            pl.BlockSpec((sbw + gw, D), fixed),
            pl.BlockSpec((1, D), fixed),
            pl.BlockSpec((D, D), fixed),
            pl.BlockSpec((M, D), lambda i: (i // per, 0)),
            pl.BlockSpec((M, D), lambda i: (i // per, 0)),
            pl.BlockSpec((D, D), fixed),
        ],
        out_specs=pl.BlockSpec((tm, D), row),
        compiler_params=_cparams("parallel"),
        name="mixout_xattn",
    )(h, sb, go, wmix, gain, wq, kmem, vmem, wo)


def _lane_vec(v):
    return jnp.zeros((1, LANES), F32).at[0, :v.shape[0]].set(v.astype(F32))


def kernel(x, mem, ffn1_norm, ffn1_w_in, ffn1_w_out, mix_norm, w_in, conv_w, a_log, dt_bias, sb_out_norm, gdn_out_norm, w_out, xattn_norm, mem_norm, xattn_w_q, xattn_w_kv, xattn_w_o, ffn2_norm, ffn2_w_in, ffn2_w_out, final_norm):
    B, S, D = x.shape
    L = ffn1_norm.shape[0]
    nh_gdn = a_log.shape[1]
    gw = nh_gdn * GDN_HEAD_DIM
    sbw = sb_out_norm.shape[1]
    M = mem.shape[1]
    assert S % 512 == 0 and (B * S) % 1024 == 0 and D % LANES == 0 and sbw % LANES == 0
    assert w_in.shape[2] == 3 * sbw + 4 * gw + 2 * nh_gdn and 2 * nh_gdn <= LANES
    T = B * S

    bf = lambda w: w.astype(BF16)
    w_mix = bf(jnp.pad(w_in, ((0, 0), (0, 0), (0, LANES - 2 * nh_gdn))))
    conv_w8 = jnp.pad(conv_w.astype(F32), ((0, 0), (0, 8 - CONV_WIDTH), (0, 0)))
    wq = bf(xattn_w_q * (XATTN_HEAD_DIM ** -0.5))
    kmem, vmem = _memkv(mem.reshape(B * M, D), mem_norm.reshape(L, 1, D), bf(xattn_w_kv))

    h = x.reshape(T, D)
    for l in range(L):
        h = _ffn(h, ffn1_norm[l][None], bf(ffn1_w_in[l]), bf(ffn1_w_out[l]))
        q, k, v, gx, z, ab = _inproj(h, mix_norm[l][None], w_mix[l], conv_w8[l], sbw=sbw, gw=gw, batch=B)
        sb = _sb_attention(q, k, v, sb_out_norm[l][None], batch=B)
        go = _gdn(gx, z, ab, _lane_vec(a_log[l]), _lane_vec(dt_bias[l]),
                  gdn_out_norm[l][None], batch=B, nheads=nh_gdn)
        h = _mixout_xattn(h, sb, go, bf(w_out[l]), xattn_norm[l][None], wq[l], kmem[l], vmem[l],
                          bf(xattn_w_o[l]), batch=B)
        h = _ffn(h, ffn2_norm[l][None], bf(ffn2_w_in[l]), bf(ffn2_w_out[l]),
                 final_norm[None] if l == L - 1 else None)
    return h.reshape(B, S, D)
```

```python
import functools
import math

import jax
import jax.numpy as jnp
from jax import lax
from jax.experimental import pallas as pl
from jax.experimental.pallas import tpu as pltpu

F32 = jnp.float32
BF16 = jnp.bfloat16

RMS_EPS = 1e-6
L2_EPS = 1e-6

LANES = 128
SB_HEAD_DIM = 64
SB_BLOCK = 128
GDN_HEAD_DIM = 128
GDN_CHUNK = 128
GDN_CHUNKS_PER_STEP = 2
CONV_WIDTH = 4
XATTN_HEAD_DIM = 256
VMEM_LIMIT = 56 * 1024 * 1024

SB_SKIP_LOG = -87.5


def _cparams(*sem):
    return pltpu.CompilerParams(dimension_semantics=sem, vmem_limit_bytes=VMEM_LIMIT)


def _rms(x, gain):
    ms = jnp.mean(x * x, axis=-1, keepdims=True)
    return x * lax.rsqrt(ms + RMS_EPS) * gain


def _dot(a, b):
    return jnp.dot(a, b, preferred_element_type=F32)


def _dot_nt(a, b):
    return lax.dot_general(a, b, (((1,), (1,)), ((), ())), preferred_element_type=F32)


def _split2(x):
    hi = x.astype(BF16)
    lo = (x - hi.astype(F32)).astype(BF16)
    return hi, lo


def _split3(x):
    hi = x.astype(BF16)
    r = x - hi.astype(F32)
    mid = r.astype(BF16)
    lo = (r - mid.astype(F32)).astype(BF16)
    return hi, mid, lo


def _lhs3(hi, lo):
    return jnp.concatenate([hi, lo], axis=1)


def _rhs3(hi, lo):
    top = jnp.concatenate([hi, lo], axis=1)
    bot = jnp.concatenate([hi, jnp.zeros_like(lo)], axis=1)
    return jnp.concatenate([top, bot], axis=0)


def _mm3(lhs3, rhs3):
    r = _dot(lhs3, rhs3)
    n = r.shape[1] // 2
    return r[:, :n] + r[:, n:]


def _softplus(x):
    return jnp.maximum(x, 0.0) + jnp.log(1.0 + jnp.exp(-jnp.abs(x)))


def _silu(x):
    return x * jax.nn.sigmoid(x)


def _ffn_kernel(x_ref, gain_ref, wg_ref, wu_ref, wo_ref, *rest, nsub):
    o_ref = rest[-1]
    ts = x_ref.shape[0] // nsub
    rows = [slice(s * ts, (s + 1) * ts) for s in range(nsub)]
    act = [None] * nsub
    for s in range(nsub + 1):
        if s < nsub:
            xn = _rms(x_ref[rows[s], :], gain_ref[...]).astype(BF16)
            g = _dot(xn, wg_ref[...])
            u = _dot(xn, wu_ref[...])
            act[s] = (_silu(g) * u).astype(BF16)
        if s >= 1:
            r = rows[s - 1]
            out = x_ref[r, :] + 0.5 * _dot(act[s - 1], wo_ref[...])
            o_ref[r, :] = _rms(out, rest[0][...]) if len(rest) == 2 else out


def _ffn(x, gain, w_in, w_out, layer, out_gain=None, *, tm=1024, nsub=4):
    T, D = x.shape
    dff = w_out.shape[1]
    resident = dict(pipeline_mode=pl.Buffered(1))
    vec = pl.BlockSpec((1, D), lambda i: (0, 0))
    tail = () if out_gain is None else (out_gain,)
    return pl.pallas_call(
        functools.partial(_ffn_kernel, nsub=nsub),
        out_shape=jax.ShapeDtypeStruct((T, D), F32),
        grid=(T // tm,),
        in_specs=[
            pl.BlockSpec((tm, D), lambda i: (i, 0)),
            vec,
            pl.BlockSpec((None, D, dff), lambda i: (layer, 0, 0), **resident),
            pl.BlockSpec((None, D, dff), lambda i: (layer, 0, 1), **resident),
            pl.BlockSpec((None, dff, D), lambda i: (layer, 0, 0), **resident),
        ] + [vec] * len(tail),
        out_specs=pl.BlockSpec((tm, D), lambda i: (i, 0)),
        compiler_params=_cparams("parallel"),
        name="ffn",
    )(x, gain, w_in, w_in, w_out, *tail)


def _inproj_kernel(x_ref, gain_ref, w_ref, cw_ref, q_ref, k_ref, v_ref, gx_ref, z_ref, ab_ref, halo_ref,
                   *, sbw, gw, tiles_per_seq, nsub):
    tm = x_ref.shape[0]
    Dh = GDN_HEAD_DIM

    @pl.when(pl.program_id(0) % tiles_per_seq == 0)
    def _():
        halo_ref[:8, :] = jnp.zeros((8, halo_ref.shape[1]), F32)

    ts = tm // nsub
    og = 3 * sbw
    oz = og + 3 * gw

    def plain(s, xn, j):
        rs = slice(s * ts, (s + 1) * ts)
        if j == 0:
            q_ref[rs, :] = (_dot(xn, w_ref[:, :sbw]) * (SB_HEAD_DIM ** -0.5)).astype(BF16)
            k_ref[rs, :] = _dot(xn, w_ref[:, sbw:2 * sbw]).astype(BF16)
        elif j == 1:
            v_ref[rs, :] = _dot(xn, w_ref[:, 2 * sbw:3 * sbw]).astype(BF16)
            z_ref[rs, :] = _dot(xn, w_ref[:, oz:oz + gw])
        else:
            ab_ref[rs, :] = _dot(xn, w_ref[:, oz + gw:oz + gw + LANES])

    def conv(s, part, graw):
        rs = slice(s * ts, (s + 1) * ts)
        cs = slice(part * gw, (part + 1) * gw)
        halo_ref[8 + s * ts:8 + (s + 1) * ts, cs] = graw
        y = graw * cw_ref[CONV_WIDTH - 1:CONV_WIDTH, cs]
        for back in range(1, CONV_WIDTH):
            shifted = halo_ref[8 + s * ts - back:8 + (s + 1) * ts - back, cs]
            y = y + shifted * cw_ref[CONV_WIDTH - 1 - back:CONV_WIDTH - back, cs]
        if s == nsub - 1:
            halo_ref[:8, cs] = graw[ts - 8:, :]
        y = _silu(y)
        if part == 2:
            gx_ref[rs, cs] = y
        else:
            for j in range(gw // Dh):
                seg = y[:, j * Dh:(j + 1) * Dh]
                seg = seg * lax.rsqrt(jnp.sum(seg * seg, axis=-1, keepdims=True) + L2_EPS)
                gx_ref[rs, part * gw + j * Dh:part * gw + (j + 1) * Dh] = seg * (Dh ** -0.5) if part == 0 else seg

    for s in range(nsub):
        xn = _rms(x_ref[s * ts:(s + 1) * ts, :], gain_ref[...]).astype(BF16)
        for part in range(3):
            graw = _dot(xn, w_ref[:, og + part * gw:og + (part + 1) * gw])
            plain(s, xn, part)
            conv(s, part, graw)


def _inproj(x, gain, w, layer, conv_w, *, sbw, gw, batch, tm=512, nsub=2):
    T, D = x.shape
    n = w.shape[2]
    row = lambda i: (i, 0)
    return pl.pallas_call(
        functools.partial(_inproj_kernel, sbw=sbw, gw=gw, tiles_per_seq=T // batch // tm, nsub=nsub),
        out_shape=[
            jax.ShapeDtypeStruct((T, sbw), BF16),
            jax.ShapeDtypeStruct((T, sbw), BF16),
            jax.ShapeDtypeStruct((T, sbw), BF16),
            jax.ShapeDtypeStruct((T, 3 * gw), F32),
            jax.ShapeDtypeStruct((T, gw), F32),
            jax.ShapeDtypeStruct((T, LANES), F32),
        ],
        grid=(T // tm,),
        in_specs=[
            pl.BlockSpec((tm, D), row),
            pl.BlockSpec((1, D), lambda i: (0, 0)),
            pl.BlockSpec((None, D, n), lambda i: (layer, 0, 0)),
            pl.BlockSpec((8, 3 * gw), lambda i: (0, 0)),
        ],
        out_specs=[
            pl.BlockSpec((tm, sbw), row),
            pl.BlockSpec((tm, sbw), row),
            pl.BlockSpec((tm, sbw), row),
            pl.BlockSpec((tm, 3 * gw), row),
            pl.BlockSpec((tm, gw), row),
            pl.BlockSpec((tm, LANES), row),
        ],
        scratch_shapes=[pltpu.VMEM((8 + tm, 3 * gw), F32)],
        compiler_params=_cparams("arbitrary"),
        name="inproj",
    )(x, gain, w, conv_w)


def _sb_kernel(q_ref, k_ref, v_ref, gain_ref, o_ref, q2_ref, carry_ref, acc_ref, *, npairs):
    blk = SB_BLOCK
    qi = pl.program_id(1)

    lane = lax.broadcasted_iota(jnp.int32, (blk, LANES), 1)
    lo_half = lane < SB_HEAD_DIM
    row = lax.broadcasted_iota(jnp.int32, (blk, blk), 0)
    col = lax.broadcasted_iota(jnp.int32, (blk, blk), 1)
    causal = jnp.concatenate([col < row] * 2, axis=0)
    m2 = jnp.concatenate([jnp.where(row > col, 1.0, 0.0), jnp.ones((blk, blk), F32)], axis=1).astype(BF16)

    def halves(x):
        zero = jnp.zeros_like(x)
        return jnp.concatenate([jnp.where(lo_half, x, zero), jnp.where(lo_half, zero, x)], axis=0)

    def visit(kbs, diagonal):
        nb = len(kbs)
        ks = [pl.multiple_of(kb * blk, blk) for kb in kbs]
        P = range(npairs)
        ls = [slice(p * LANES, (p + 1) * LANES) for p in P]
        if diagonal:
            for p in P:
                q2_ref[p] = halves(q_ref[:, ls[p]])
        z = [[_dot_nt(q2_ref[p], k_ref[pl.ds(ks[b], blk), ls[p]]) for p in P] for b in range(nb)]
        log_beta = [[None] * npairs for _ in range(nb)]
        cs = [[None] * npairs for _ in range(nb)]
        for b in range(nb):
            for p in P:
                sp = jnp.log(1.0 + jnp.exp(-jnp.abs(z[b][p])))
                lb = jnp.minimum(z[b][p], 0.0) - sp
                log_1m = lb - z[b][p]
                if diagonal and b == 0:
                    log_1m = jnp.where(causal, log_1m, 0.0)
                log_beta[b][p] = lb
                cs[b][p] = _dot(log_1m.astype(BF16), m2)
        for p in P:
            carry = None if diagonal else carry_ref[p]
            contrib = None
            for b in range(nb):
                between = cs[b][p][:, :blk] if carry is None else cs[b][p][:, :blk] + carry
                carry = cs[b][p][:, blk:] if carry is None else carry + cs[b][p][:, blk:]
                a = jnp.exp(log_beta[b][p] + between)
                if diagonal and b == 0:
                    a = jnp.where(causal, a, 0.0)
                a = a.astype(BF16)
                c = _dot(jnp.concatenate([a[:blk], a[blk:]], axis=1),
                         halves(v_ref[pl.ds(ks[b], blk), ls[p]]))
                contrib = c if contrib is None else contrib + c
            carry_ref[p] = carry
            if diagonal:
                acc_ref[:, ls[p]] = contrib
            else:
                acc_ref[:, ls[p]] += contrib

    def live():
        m = carry_ref[0]
        for p in range(1, npairs):
            m = jnp.maximum(m, carry_ref[p])
        return (jnp.max(m) > SB_SKIP_LOG).astype(jnp.int32)

    @pl.when(qi == 0)
    def _():
        visit([qi], True)

    @pl.when(qi > 0)
    def _():
        visit([qi, qi - 1], True)

    def cond(c):
        kb, go = c
        return jnp.logical_and(kb >= 0, go > 0)

    def body(c):
        kb, _ = c
        visit([kb], False)
        return kb - 1, live()

    lax.while_loop(cond, body, (qi - 2, live()))

    r = lax.broadcasted_iota(jnp.int32, (LANES, LANES), 0) // SB_HEAD_DIM
    c = lax.broadcasted_iota(jnp.int32, (LANES, LANES), 1) // SB_HEAD_DIM
    ones_blk = jnp.where(r == c, 1.0, 0.0).astype(BF16)
    ones2 = jnp.concatenate([ones_blk, ones_blk], axis=0)
    for p in range(npairs):
        ls = slice(p * LANES, (p + 1) * LANES)
        o = acc_ref[:, ls]
        hi, lo = _split2(o * o)
        ms = _dot(_lhs3(hi, lo), ones2) * (1.0 / SB_HEAD_DIM)
        o_ref[:, ls] = (o * lax.rsqrt(ms + RMS_EPS) * gain_ref[:, ls]).astype(BF16)


def _sb_attention(q, k, v, gain, *, batch):
    T, W = q.shape
    S = T // batch
    nq = S // SB_BLOCK
    npairs = W // LANES
    return pl.pallas_call(
        functools.partial(_sb_kernel, npairs=npairs),
        out_shape=jax.ShapeDtypeStruct((T, W), BF16),
        grid=(batch, nq),
        in_specs=[
            pl.BlockSpec((SB_BLOCK, W), lambda b, i: (b * nq + i, 0)),
            pl.BlockSpec((S, W), lambda b, i: (b, 0)),
            pl.BlockSpec((S, W), lambda b, i: (b, 0)),
            pl.BlockSpec((1, W), lambda b, i: (0, 0)),
        ],
        out_specs=pl.BlockSpec((SB_BLOCK, W), lambda b, i: (b * nq + i, 0)),
        scratch_shapes=[
            pltpu.VMEM((npairs, 2 * SB_BLOCK, LANES), BF16),
            pltpu.VMEM((npairs, 2 * SB_BLOCK, SB_BLOCK), F32),
            pltpu.VMEM((SB_BLOCK, W), F32),
        ],
        compiler_params=_cparams("parallel", "arbitrary"),
        name="sb_attention",
    )(q, k, v, gain)


def _gdn_kernel(x_ref, z_ref, ab_ref, alog_ref, dtb_ref, gnorm_ref, o_ref, state_ref, *, nheads, nchunks):
    C = GDN_CHUNK
    Dh = GDN_HEAD_DIM
    gw = nheads * Dh

    @pl.when(pl.program_id(1) == 0)
    def _():
        state_ref[...] = jnp.zeros_like(state_ref)

    ab = ab_ref[...]
    g_raw = -jnp.exp(alog_ref[...]) * _softplus(ab + dtb_ref[...])
    beta_all = jax.nn.sigmoid(ab)

    row = lax.broadcasted_iota(jnp.int32, (C, C), 0)
    col = lax.broadcasted_iota(jnp.int32, (C, C), 1)
    tril = col <= row
    strict = col < row
    eye = jnp.where(row == col, 1.0, 0.0)
    tril_b = jnp.where(tril, 1.0, 0.0).astype(BF16)

    gc, gct = [], []
    for c in range(nchunks):
        g1, g2, g3 = _split3(g_raw[c * C:(c + 1) * C, :])
        gc3 = _dot(tril_b, jnp.concatenate([g1, g2, g3], axis=1))
        gc.append(gc3[:, :LANES] + (gc3[:, LANES:2 * LANES] + gc3[:, 2 * LANES:]))
        gct.append(gc[c].T)

    units = [(c, h) for c in range(nchunks) for h in range(nheads)]
    U = range(len(units))
    q, k, v, gcol, beta, glast, decay, kb = ([None] * len(units) for _ in range(8))
    for i, (c, h) in enumerate(units):
        rs = slice(c * C, (c + 1) * C)
        q[i] = x_ref[rs, h * Dh:(h + 1) * Dh]
        k[i] = x_ref[rs, gw + h * Dh:gw + (h + 1) * Dh]
        v[i] = x_ref[rs, 2 * gw + h * Dh:2 * gw + (h + 1) * Dh]
        gcol[i] = gc[c][:, h:h + 1]
        grow = gct[c][h:h + 1, :]
        beta[i] = beta_all[rs, nheads + h:nheads + h + 1]
        glast[i] = gc[c][C - 1:C, h:h + 1]
        decay[i] = jnp.where(tril, jnp.exp(jnp.where(tril, gcol[i] - grow, 0.0)), 0.0)
        kb[i] = k[i] * beta[i]

    kk_qk = [_dot_nt(jnp.concatenate([kb[i].astype(BF16), q[i].astype(BF16)], axis=0), k[i].astype(BF16))
             for i in U]
    a = [jnp.where(strict, -(kk_qk[i][:C] * decay[i]), 0.0) for i in U]
    qk = [jnp.where(tril, kk_qk[i][C:] * decay[i], 0.0).astype(BF16) for i in U]

    tinv = [eye + a[i] for i in U]
    psplit = [_split2(a[i]) for i in U]
    pw = [_mm3(_lhs3(*psplit[i]), _rhs3(*psplit[i])) for i in U]
    levels = int(math.log2(C)) - 1
    for lvl in range(levels):
        psplit = [_split2(pw[i]) for i in U]
        tsplit = [_split2(tinv[i]) for i in U]
        if lvl < levels - 1:
            r = [_mm3(jnp.concatenate([_lhs3(*psplit[i]), _lhs3(*tsplit[i])], axis=0), _rhs3(*psplit[i]))
                 for i in U]
            pw = [r[i][:C] for i in U]
            tinv = [tinv[i] + r[i][C:] for i in U]
        else:
            tinv = [tinv[i] + _mm3(_lhs3(*tsplit[i]), _rhs3(*psplit[i])) for i in U]

    eg = [jnp.exp(gcol[i]) for i in U]
    uw = [_dot(tinv[i].astype(BF16), jnp.concatenate([v[i] * beta[i], kb[i] * eg[i]], axis=1).astype(BF16))
          for i in U]
    wq16 = [jnp.concatenate([uw[i][:, Dh:].astype(BF16), (q[i] * eg[i]).astype(BF16)], axis=0) for i in U]
    k_end_t = [(k[i] * jnp.exp(glast[i] - gcol[i])).T.astype(BF16) for i in U]

    st = [state_ref[h] for h in range(nheads)]
    for c in range(nchunks):
        ids = [c * nheads + h for h in range(nheads)]
        ws_qs = [_dot(wq16[i], st[h].astype(BF16)) for h, i in enumerate(ids)]
        vn16 = [(uw[i][:, :Dh] - ws_qs[h][:C]).astype(BF16) for h, i in enumerate(ids)]
        ov_sv = [_dot(jnp.concatenate([qk[i], k_end_t[i]], axis=0), vn16[h])
                 for h, i in enumerate(ids)]
        for h, i in enumerate(ids):
            o = ws_qs[h][C:] + ov_sv[h][:C]
            st[h] = st[h] * jnp.exp(glast[i]) + ov_sv[h][C:]
            zz = z_ref[c * C:(c + 1) * C, h * Dh:(h + 1) * Dh]
            o_ref[c * C:(c + 1) * C, h * Dh:(h + 1) * Dh] = (_rms(o, gnorm_ref[...]) * _silu(zz)).astype(BF16)
    for h in range(nheads):
        state_ref[h] = st[h]


def _gdn(gx, z, ab, alog, dtb, gnorm, *, batch, nheads, nchunks=GDN_CHUNKS_PER_STEP):
    T, gw3 = gx.shape
    gw = gw3 // 3
    S = T // batch
    C = nchunks * GDN_CHUNK
    nc = S // C
    row = lambda b, c: (b * nc + c, 0)
    fixed = lambda b, c: (0, 0)
    return pl.pallas_call(
        functools.partial(_gdn_kernel, nheads=nheads, nchunks=nchunks),
        out_shape=jax.ShapeDtypeStruct((T, gw), BF16),
        grid=(batch, nc),
        in_specs=[
            pl.BlockSpec((C, gw3), row),
            pl.BlockSpec((C, gw), row),
            pl.BlockSpec((C, LANES), row),
            pl.BlockSpec((1, LANES), fixed),
            pl.BlockSpec((1, LANES), fixed),
            pl.BlockSpec((1, GDN_HEAD_DIM), fixed),
        ],
        out_specs=pl.BlockSpec((C, gw), row),
        scratch_shapes=[pltpu.VMEM((nheads, GDN_HEAD_DIM, GDN_HEAD_DIM), F32)],
        compiler_params=_cparams("parallel", "arbitrary"),
        name="gdn",
    )(gx, z, ab, alog, dtb, gnorm)


def _memkv_kernel(m_ref, gain_ref, w_ref, k_ref, v_ref, *, d):
    mn = _rms(m_ref[...], gain_ref[0]).astype(BF16)
    k_ref[0] = _dot(mn, w_ref[0, :, :d]).astype(BF16)
    v_ref[0] = _dot(mn, w_ref[0, :, d:]).astype(BF16)


def _memkv(mem2d, gains, w_kv):
    R, D = mem2d.shape
    L = w_kv.shape[0]
    return pl.pallas_call(
        functools.partial(_memkv_kernel, d=D),
        out_shape=[jax.ShapeDtypeStruct((L, R, D), BF16), jax.ShapeDtypeStruct((L, R, D), BF16)],
        grid=(L,),
        in_specs=[
            pl.BlockSpec((R, D), lambda l: (0, 0)),
            pl.BlockSpec((1, 1, D), lambda l: (l, 0, 0)),
            pl.BlockSpec((1, D, 2 * D), lambda l: (l, 0, 0)),
        ],
        out_specs=[
            pl.BlockSpec((1, R, D), lambda l: (l, 0, 0)),
            pl.BlockSpec((1, R, D), lambda l: (l, 0, 0)),
        ],
        compiler_params=_cparams("parallel"),
        name="memkv",
    )(mem2d, gains, w_kv)


def _mixout_xattn_kernel(h_ref, sb_ref, go_ref, wmix_ref, gain_ref, wq_ref, k_ref, v_ref, wo_ref, o_ref,
                        *, nheads, nsub, sbw):
    Dh = XATTN_HEAD_DIM
    ts = h_ref.shape[0] // nsub
    R = range(nsub)
    H = range(nheads)
    rows = [slice(s * ts, (s + 1) * ts) for s in R]
    ls = [slice(hd * Dh, (hd + 1) * Dh) for hd in H]
    x = [h_ref[rows[s], :] + (_dot(sb_ref[rows[s], :], wmix_ref[:sbw, :]) + _dot(go_ref[rows[s], :], wmix_ref[sbw:, :]))
         for s in R]
    q = [_dot(_rms(x[s], gain_ref[...]).astype(BF16), wq_ref[...]).astype(BF16) for s in R]
    sc = [[_dot_nt(q[s][:, ls[hd]], k_ref[:, ls[hd]]) for hd in H] for s in R]
    p = [[jnp.exp(sc[s][hd] - jnp.max(sc[s][hd], axis=-1, keepdims=True)) for hd in H] for s in R]
    pv = [[_dot(p[s][hd].astype(BF16), v_ref[:, ls[hd]]) for hd in H] for s in R]
    for s in R:
        o = jnp.concatenate([(pv[s][hd] / jnp.sum(p[s][hd], axis=-1, keepdims=True)).astype(BF16) for hd in H],
                            axis=-1)
        o_ref[rows[s], :] = x[s] + _dot(o, wo_ref[...])


def _mixout_xattn(h, sb, go, wmix, gain, wq, kmem, vmem, wo, layer, *, batch, tm=512, nsub=2):
    T, D = h.shape
    S = T // batch
    per = S // tm
    M = kmem.shape[1] // batch
    sbw = sb.shape[1]
    gw = go.shape[1]
    row = lambda i: (i, 0)
    fixed = lambda i: (0, 0)
    weight = lambda i: (layer, 0, 0)
    mem_rows = lambda i: (layer, i // per, 0)
    return pl.pallas_call(
        functools.partial(_mixout_xattn_kernel, nheads=D // XATTN_HEAD_DIM, nsub=nsub, sbw=sbw),
        out_shape=jax.ShapeDtypeStruct((T, D), F32),
        grid=(T // tm,),
        in_specs=[
            pl.BlockSpec((tm, D), row),
            pl.BlockSpec((tm, sbw), row),
            pl.BlockSpec((tm, gw), row),
            pl.BlockSpec((None, sbw + gw, D), weight),
            pl.BlockSpec((1, D), fixed),
            pl.BlockSpec((None, D, D), weight),
            pl.BlockSpec((None, M, D), mem_rows),
            pl.BlockSpec((None, M, D), mem_rows),
            pl.BlockSpec((None, D, D), weight),
        ],
        out_specs=pl.BlockSpec((tm, D), row),
        compiler_params=_cparams("parallel"),
        name="mixout_xattn",
    )(h, sb, go, wmix, gain, wq, kmem, vmem, wo)


def _lane_vec(v):
    return jnp.zeros((1, LANES), F32).at[0, :v.shape[0]].set(v.astype(F32))


def kernel(x, mem, ffn1_norm, ffn1_w_in, ffn1_w_out, mix_norm, w_in, conv_w, a_log, dt_bias, sb_out_norm, gdn_out_norm, w_out, xattn_norm, mem_norm, xattn_w_q, xattn_w_kv, xattn_w_o, ffn2_norm, ffn2_w_in, ffn2_w_out, final_norm):
    B, S, D = x.shape
    L = ffn1_norm.shape[0]
    nh_gdn = a_log.shape[1]
    gw = nh_gdn * GDN_HEAD_DIM
    sbw = sb_out_norm.shape[1]
    M = mem.shape[1]
    assert S % 512 == 0 and (B * S) % 1024 == 0 and D % LANES == 0 and sbw % LANES == 0
    assert w_in.shape[2] == 3 * sbw + 4 * gw + 2 * nh_gdn and 2 * nh_gdn <= LANES
    T = B * S

    bf = lambda w: w.astype(BF16)
    ffn1_wi, ffn1_wo, ffn2_wi, ffn2_wo = bf(ffn1_w_in), bf(ffn1_w_out), bf(ffn2_w_in), bf(ffn2_w_out)
    w_mix = bf(jnp.pad(w_in, ((0, 0), (0, 0), (0, LANES - 2 * nh_gdn))))
    w_mixout, w_xo = bf(w_out), bf(xattn_w_o)
    conv_w8 = jnp.pad(conv_w.astype(F32), ((0, 0), (0, 8 - CONV_WIDTH), (0, 0)))
    wq = bf(xattn_w_q * (XATTN_HEAD_DIM ** -0.5))
    kmem, vmem = _memkv(mem.reshape(B * M, D), mem_norm.reshape(L, 1, D), bf(xattn_w_kv))

    h = x.reshape(T, D)
    for l in range(L):
        h = _ffn(h, ffn1_norm[l][None], ffn1_wi, ffn1_wo, l)
        q, k, v, gx, z, ab = _inproj(h, mix_norm[l][None], w_mix, l, conv_w8[l], sbw=sbw, gw=gw, batch=B)
        sb = _sb_attention(q, k, v, sb_out_norm[l][None], batch=B)
        go = _gdn(gx, z, ab, _lane_vec(a_log[l]), _lane_vec(dt_bias[l]),
                  gdn_out_norm[l][None], batch=B, nheads=nh_gdn)
        h = _mixout_xattn(h, sb, go, w_mixout, xattn_norm[l][None], wq, kmem, vmem, w_xo, l, batch=B)
        h = _ffn(h, ffn2_norm[l][None], ffn2_wi, ffn2_wo, l, final_norm[None] if l == L - 1 else None)
    return h.reshape(B, S, D)
```

```python
import functools
import math

import jax
import jax.numpy as jnp
from jax import lax
from jax.experimental import pallas as pl
from jax.experimental.pallas import tpu as pltpu

F32 = jnp.float32
BF16 = jnp.bfloat16

RMS_EPS = 1e-6
L2_EPS = 1e-6

LANES = 128
SB_HEAD_DIM = 64
SB_BLOCK = 128
GDN_HEAD_DIM = 128
GDN_CHUNK = 128
GDN_CHUNKS_PER_STEP = 4
CONV_WIDTH = 4
XATTN_HEAD_DIM = 256
VMEM_LIMIT = 56 * 1024 * 1024

SB_SKIP_LOG = -87.5


def _cparams(*sem):
    return pltpu.CompilerParams(dimension_semantics=sem, vmem_limit_bytes=VMEM_LIMIT)


def _rms(x, gain):
    ms = jnp.mean(x * x, axis=-1, keepdims=True)
    return x * lax.rsqrt(ms + RMS_EPS) * gain


def _dot(a, b):
    return jnp.dot(a, b, preferred_element_type=F32)


def _dot_nt(a, b):
    return lax.dot_general(a, b, (((1,), (1,)), ((), ())), preferred_element_type=F32)


def _split2(x):
    hi = x.astype(BF16)
    lo = (x - hi.astype(F32)).astype(BF16)
    return hi, lo


def _split3(x):
    hi = x.astype(BF16)
    r = x - hi.astype(F32)
    mid = r.astype(BF16)
    lo = (r - mid.astype(F32)).astype(BF16)
    return hi, mid, lo


def _lhs3(hi, lo):
    return jnp.concatenate([hi, lo], axis=1)


def _rhs3(hi, lo):
    top = jnp.concatenate([hi, lo], axis=1)
    bot = jnp.concatenate([hi, jnp.zeros_like(lo)], axis=1)
    return jnp.concatenate([top, bot], axis=0)


def _mm3(lhs3, rhs3):
    r = _dot(lhs3, rhs3)
    n = r.shape[1] // 2
    return r[:, :n] + r[:, n:]


def _softplus(x):
    return jnp.maximum(x, 0.0) + jnp.log(1.0 + jnp.exp(-jnp.abs(x)))


def _silu(x):
    return x * jax.nn.sigmoid(x)


def _ffn_kernel(x_ref, gain_ref, wg_ref, wu_ref, wo_ref, *rest, nsub):
    o_ref = rest[-1]
    ts = x_ref.shape[0] // nsub
    rows = [slice(s * ts, (s + 1) * ts) for s in range(nsub)]
    act = [None] * nsub
    for s in range(nsub + 1):
        if s < nsub:
            xn = _rms(x_ref[rows[s], :], gain_ref[...]).astype(BF16)
            g = _dot(xn, wg_ref[...])
            u = _dot(xn, wu_ref[...])
            act[s] = (_silu(g) * u).astype(BF16)
        if s >= 1:
            r = rows[s - 1]
            out = x_ref[r, :] + 0.5 * _dot(act[s - 1], wo_ref[...])
            o_ref[r, :] = _rms(out, rest[0][...]) if len(rest) == 2 else out


def _ffn(x, gain, w_in, w_out, layer, out_gain=None, *, tm=1024, nsub=4):
    T, D = x.shape
    dff = w_out.shape[1]
    resident = dict(pipeline_mode=pl.Buffered(1))
    vec = pl.BlockSpec((1, D), lambda i: (0, 0))
    tail = () if out_gain is None else (out_gain,)
    return pl.pallas_call(
        functools.partial(_ffn_kernel, nsub=nsub),
        out_shape=jax.ShapeDtypeStruct((T, D), F32),
        grid=(T // tm,),
        in_specs=[
            pl.BlockSpec((tm, D), lambda i: (i, 0)),
            vec,
            pl.BlockSpec((None, D, dff), lambda i: (layer, 0, 0), **resident),
            pl.BlockSpec((None, D, dff), lambda i: (layer, 0, 1), **resident),
            pl.BlockSpec((None, dff, D), lambda i: (layer, 0, 0), **resident),
        ] + [vec] * len(tail),
        out_specs=pl.BlockSpec((tm, D), lambda i: (i, 0)),
        compiler_params=_cparams("parallel"),
        name="ffn",
    )(x, gain, w_in, w_in, w_out, *tail)


def _inproj_kernel(x_ref, gain_ref, w_ref, cw_ref, q_ref, k_ref, v_ref, gx_ref, z_ref, ab_ref, halo_ref,
                   *, sbw, gw, tiles_per_seq, nsub):
    tm = x_ref.shape[0]
    Dh = GDN_HEAD_DIM

    @pl.when(pl.program_id(0) % tiles_per_seq == 0)
    def _():
        halo_ref[:8, :] = jnp.zeros((8, halo_ref.shape[1]), F32)

    ts = tm // nsub
    og = 3 * sbw
    oz = og + 3 * gw

    def plain(s, xn, j):
        rs = slice(s * ts, (s + 1) * ts)
        if j == 0:
            q_ref[rs, :] = (_dot(xn, w_ref[:, :sbw]) * (SB_HEAD_DIM ** -0.5)).astype(BF16)
            k_ref[rs, :] = _dot(xn, w_ref[:, sbw:2 * sbw]).astype(BF16)
        elif j == 1:
            v_ref[rs, :] = _dot(xn, w_ref[:, 2 * sbw:3 * sbw]).astype(BF16)
            z_ref[rs, :] = _dot(xn, w_ref[:, oz:oz + gw])
        else:
            ab_ref[rs, :] = _dot(xn, w_ref[:, oz + gw:oz + gw + LANES])

    def conv(s, part, graw):
        rs = slice(s * ts, (s + 1) * ts)
        cs = slice(part * gw, (part + 1) * gw)
        halo_ref[8 + s * ts:8 + (s + 1) * ts, cs] = graw
        y = graw * cw_ref[CONV_WIDTH - 1:CONV_WIDTH, cs]
        for back in range(1, CONV_WIDTH):
            shifted = halo_ref[8 + s * ts - back:8 + (s + 1) * ts - back, cs]
            y = y + shifted * cw_ref[CONV_WIDTH - 1 - back:CONV_WIDTH - back, cs]
        if s == nsub - 1:
            halo_ref[:8, cs] = graw[ts - 8:, :]
        y = _silu(y)
        if part == 2:
            gx_ref[rs, cs] = y
        else:
            for j in range(gw // Dh):
                seg = y[:, j * Dh:(j + 1) * Dh]
                seg = seg * lax.rsqrt(jnp.sum(seg * seg, axis=-1, keepdims=True) + L2_EPS)
                gx_ref[rs, part * gw + j * Dh:part * gw + (j + 1) * Dh] = seg * (Dh ** -0.5) if part == 0 else seg

    for s in range(nsub):
        xn = _rms(x_ref[s * ts:(s + 1) * ts, :], gain_ref[...]).astype(BF16)
        for part in range(3):
            graw = _dot(xn, w_ref[:, og + part * gw:og + (part + 1) * gw])
            plain(s, xn, part)
            conv(s, part, graw)


def _inproj(x, gain, w, layer, conv_w, *, sbw, gw, batch, tm=512, nsub=2):
    T, D = x.shape
    n = w.shape[2]
    row = lambda i: (i, 0)
    return pl.pallas_call(
        functools.partial(_inproj_kernel, sbw=sbw, gw=gw, tiles_per_seq=T // batch // tm, nsub=nsub),
        out_shape=[
            jax.ShapeDtypeStruct((T, sbw), BF16),
            jax.ShapeDtypeStruct((T, sbw), BF16),
            jax.ShapeDtypeStruct((T, sbw), BF16),
            jax.ShapeDtypeStruct((T, 3 * gw), F32),
            jax.ShapeDtypeStruct((T, gw), F32),
            jax.ShapeDtypeStruct((T, LANES), F32),
        ],
        grid=(T // tm,),
        in_specs=[
            pl.BlockSpec((tm, D), row),
            pl.BlockSpec((1, D), lambda i: (0, 0)),
            pl.BlockSpec((None, D, n), lambda i: (layer, 0, 0)),
            pl.BlockSpec((8, 3 * gw), lambda i: (0, 0)),
        ],
        out_specs=[
            pl.BlockSpec((tm, sbw), row),
            pl.BlockSpec((tm, sbw), row),
            pl.BlockSpec((tm, sbw), row),
            pl.BlockSpec((tm, 3 * gw), row),
            pl.BlockSpec((tm, gw), row),
            pl.BlockSpec((tm, LANES), row),
        ],
        scratch_shapes=[pltpu.VMEM((8 + tm, 3 * gw), F32)],
        compiler_params=_cparams("arbitrary"),
        name="inproj",
    )(x, gain, w, conv_w)


def _sb_kernel(q_ref, k_ref, v_ref, gain_ref, o_ref, q2_ref, carry_ref, acc_ref, *, npairs):
    blk = SB_BLOCK
    qi = pl.program_id(1)

    lane = lax.broadcasted_iota(jnp.int32, (blk, LANES), 1)
    lo_half = lane < SB_HEAD_DIM
    row = lax.broadcasted_iota(jnp.int32, (blk, blk), 0)
    col = lax.broadcasted_iota(jnp.int32, (blk, blk), 1)
    causal = jnp.concatenate([col < row] * 2, axis=0)
    m2 = jnp.concatenate([jnp.where(row > col, 1.0, 0.0), jnp.ones((blk, blk), F32)], axis=1).astype(BF16)

    def halves(x):
        zero = jnp.zeros_like(x)
        return jnp.concatenate([jnp.where(lo_half, x, zero), jnp.where(lo_half, zero, x)], axis=0)

    def visit(kbs, diagonal):
        nb = len(kbs)
        ks = [pl.multiple_of(kb * blk, blk) for kb in kbs]
        P = range(npairs)
        ls = [slice(p * LANES, (p + 1) * LANES) for p in P]
        if diagonal:
            for p in P:
                q2_ref[p] = halves(q_ref[:, ls[p]])
        z = [[_dot_nt(q2_ref[p], k_ref[pl.ds(ks[b], blk), ls[p]]) for p in P] for b in range(nb)]
        log_beta = [[None] * npairs for _ in range(nb)]
        cs = [[None] * npairs for _ in range(nb)]
        for b in range(nb):
            for p in P:
                sp = jnp.log(1.0 + jnp.exp(-jnp.abs(z[b][p])))
                lb = jnp.minimum(z[b][p], 0.0) - sp
                log_1m = lb - z[b][p]
                if diagonal and b == 0:
                    log_1m = jnp.where(causal, log_1m, 0.0)
                log_beta[b][p] = lb
                cs[b][p] = _dot(log_1m.astype(BF16), m2)
        for p in P:
            carry = None if diagonal else carry_ref[p]
            contrib = None
            for b in range(nb):
                between = cs[b][p][:, :blk] if carry is None else cs[b][p][:, :blk] + carry
                carry = cs[b][p][:, blk:] if carry is None else carry + cs[b][p][:, blk:]
                a = jnp.exp(log_beta[b][p] + between)
                if diagonal and b == 0:
                    a = jnp.where(causal, a, 0.0)
                a = a.astype(BF16)
                c = _dot(jnp.concatenate([a[:blk], a[blk:]], axis=1),
                         halves(v_ref[pl.ds(ks[b], blk), ls[p]]))
                contrib = c if contrib is None else contrib + c
            carry_ref[p] = carry
            if diagonal:
                acc_ref[:, ls[p]] = contrib
            else:
                acc_ref[:, ls[p]] += contrib

    def live():
        m = carry_ref[0]
        for p in range(1, npairs):
            m = jnp.maximum(m, carry_ref[p])
        return (jnp.max(m) > SB_SKIP_LOG).astype(jnp.int32)

    @pl.when(qi == 0)
    def _():
        visit([qi], True)

    @pl.when(qi > 0)
    def _():
        visit([qi, qi - 1], True)

    def cond(c):
        kb, go = c
        return jnp.logical_and(kb >= 0, go > 0)

    def body(c):
        kb, _ = c
        visit([kb], False)
        return kb - 1, live()

    lax.while_loop(cond, body, (qi - 2, live()))

    r = lax.broadcasted_iota(jnp.int32, (LANES, LANES), 0) // SB_HEAD_DIM
    c = lax.broadcasted_iota(jnp.int32, (LANES, LANES), 1) // SB_HEAD_DIM
    ones_blk = jnp.where(r == c, 1.0, 0.0).astype(BF16)
    ones2 = jnp.concatenate([ones_blk, ones_blk], axis=0)
    for p in range(npairs):
        ls = slice(p * LANES, (p + 1) * LANES)
        o = acc_ref[:, ls]
        hi, lo = _split2(o * o)
        ms = _dot(_lhs3(hi, lo), ones2) * (1.0 / SB_HEAD_DIM)
        o_ref[:, ls] = (o * lax.rsqrt(ms + RMS_EPS) * gain_ref[:, ls]).astype(BF16)


def _sb_attention(q, k, v, gain, *, batch):
    T, W = q.shape
    S = T // batch
    nq = S // SB_BLOCK
    npairs = W // LANES
    return pl.pallas_call(
        functools.partial(_sb_kernel, npairs=npairs),
        out_shape=jax.ShapeDtypeStruct((T, W), BF16),
        grid=(batch, nq),
        in_specs=[
            pl.BlockSpec((SB_BLOCK, W), lambda b, i: (b * nq + i, 0)),
            pl.BlockSpec((S, W), lambda b, i: (b, 0)),
            pl.BlockSpec((S, W), lambda b, i: (b, 0)),
            pl.BlockSpec((1, W), lambda b, i: (0, 0)),
        ],
        out_specs=pl.BlockSpec((SB_BLOCK, W), lambda b, i: (b * nq + i, 0)),
        scratch_shapes=[
            pltpu.VMEM((npairs, 2 * SB_BLOCK, LANES), BF16),
            pltpu.VMEM((npairs, 2 * SB_BLOCK, SB_BLOCK), F32),
            pltpu.VMEM((SB_BLOCK, W), F32),
        ],
        compiler_params=_cparams("parallel", "arbitrary"),
        name="sb_attention",
    )(q, k, v, gain)


def _gdn_kernel(x_ref, z_ref, ab_ref, alog_ref, dtb_ref, gnorm_ref, o_ref, state_ref, *, nheads, nchunks):
    C = GDN_CHUNK
    Dh = GDN_HEAD_DIM
    gw = nheads * Dh

    @pl.when(pl.program_id(1) == 0)
    def _():
        state_ref[...] = jnp.zeros_like(state_ref)

    ab = ab_ref[...]
    g_raw = -jnp.exp(alog_ref[...]) * _softplus(ab + dtb_ref[...])
    beta_all = jax.nn.sigmoid(ab)

    row = lax.broadcasted_iota(jnp.int32, (C, C), 0)
    col = lax.broadcasted_iota(jnp.int32, (C, C), 1)
    tril = col <= row
    strict = col < row
    eye = jnp.where(row == col, 1.0, 0.0)
    tril_b = jnp.where(tril, 1.0, 0.0).astype(BF16)

    gc, gct = [], []
    for c in range(nchunks):
        g1, g2, g3 = _split3(g_raw[c * C:(c + 1) * C, :])
        gc3 = _dot(tril_b, jnp.concatenate([g1, g2, g3], axis=1))
        gc.append(gc3[:, :LANES] + (gc3[:, LANES:2 * LANES] + gc3[:, 2 * LANES:]))
        gct.append(gc[c].T)

    units = [(c, h) for c in range(nchunks) for h in range(nheads)]
    U = range(len(units))
    q, k, v, gcol, beta, glast, decay, kb = ([None] * len(units) for _ in range(8))
    for i, (c, h) in enumerate(units):
        rs = slice(c * C, (c + 1) * C)
        q[i] = x_ref[rs, h * Dh:(h + 1) * Dh]
        k[i] = x_ref[rs, gw + h * Dh:gw + (h + 1) * Dh]
        v[i] = x_ref[rs, 2 * gw + h * Dh:2 * gw + (h + 1) * Dh]
        gcol[i] = gc[c][:, h:h + 1]
        grow = gct[c][h:h + 1, :]
        beta[i] = beta_all[rs, nheads + h:nheads + h + 1]
        glast[i] = gc[c][C - 1:C, h:h + 1]
        decay[i] = jnp.where(tril, jnp.exp(jnp.where(tril, gcol[i] - grow, 0.0)), 0.0)
        kb[i] = k[i] * beta[i]

    kk_qk = [_dot_nt(jnp.concatenate([kb[i].astype(BF16), q[i].astype(BF16)], axis=0), k[i].astype(BF16))
             for i in U]
    a = [jnp.where(strict, -(kk_qk[i][:C] * decay[i]), 0.0) for i in U]
    qk = [jnp.where(tril, kk_qk[i][C:] * decay[i], 0.0).astype(BF16) for i in U]

    tinv = [eye + a[i] for i in U]
    psplit = [_split2(a[i]) for i in U]
    pw = [_mm3(_lhs3(*psplit[i]), _rhs3(*psplit[i])) for i in U]
    levels = int(math.log2(C)) - 1
    for lvl in range(levels):
        psplit = [_split2(pw[i]) for i in U]
        tsplit = [_split2(tinv[i]) for i in U]
        if lvl < levels - 1:
            r = [_mm3(jnp.concatenate([_lhs3(*psplit[i]), _lhs3(*tsplit[i])], axis=0), _rhs3(*psplit[i]))
                 for i in U]
            pw = [r[i][:C] for i in U]
            tinv = [tinv[i] + r[i][C:] for i in U]
        else:
            tinv = [tinv[i] + _mm3(_lhs3(*tsplit[i]), _rhs3(*psplit[i])) for i in U]

    eg = [jnp.exp(gcol[i]) for i in U]
    uw = [_dot(tinv[i].astype(BF16), jnp.concatenate([v[i] * beta[i], kb[i] * eg[i]], axis=1).astype(BF16))
          for i in U]
    wq16 = [jnp.concatenate([uw[i][:, Dh:].astype(BF16), (q[i] * eg[i]).astype(BF16)], axis=0) for i in U]
    k_end_t = [(k[i] * jnp.exp(glast[i] - gcol[i])).T.astype(BF16) for i in U]

    st = [state_ref[h] for h in range(nheads)]
    for c in range(nchunks):
        ids = [c * nheads + h for h in range(nheads)]
        ws_qs = [_dot(wq16[i], st[h].astype(BF16)) for h, i in enumerate(ids)]
        vn16 = [(uw[i][:, :Dh] - ws_qs[h][:C]).astype(BF16) for h, i in enumerate(ids)]
        ov_sv = [_dot(jnp.concatenate([qk[i], k_end_t[i]], axis=0), vn16[h])
                 for h, i in enumerate(ids)]
        for h, i in enumerate(ids):
            o = ws_qs[h][C:] + ov_sv[h][:C]
            st[h] = st[h] * jnp.exp(glast[i]) + ov_sv[h][C:]
            zz = z_ref[c * C:(c + 1) * C, h * Dh:(h + 1) * Dh]
            o_ref[c * C:(c + 1) * C, h * Dh:(h + 1) * Dh] = (_rms(o, gnorm_ref[...]) * _silu(zz)).astype(BF16)
    for h in range(nheads):
        state_ref[h] = st[h]


def _gdn(gx, z, ab, alog, dtb, gnorm, *, batch, nheads, nchunks=GDN_CHUNKS_PER_STEP):
    T, gw3 = gx.shape
    gw = gw3 // 3
    S = T // batch
    C = nchunks * GDN_CHUNK
    nc = S // C
    row = lambda b, c: (b * nc + c, 0)
    fixed = lambda b, c: (0, 0)
    return pl.pallas_call(
        functools.partial(_gdn_kernel, nheads=nheads, nchunks=nchunks),
        out_shape=jax.ShapeDtypeStruct((T, gw), BF16),
        grid=(batch, nc),
        in_specs=[
            pl.BlockSpec((C, gw3), row),
            pl.BlockSpec((C, gw), row),
            pl.BlockSpec((C, LANES), row),
            pl.BlockSpec((1, LANES), fixed),
            pl.BlockSpec((1, LANES), fixed),
            pl.BlockSpec((1, GDN_HEAD_DIM), fixed),
        ],
        out_specs=pl.BlockSpec((C, gw), row),
        scratch_shapes=[pltpu.VMEM((nheads, GDN_HEAD_DIM, GDN_HEAD_DIM), F32)],
        compiler_params=_cparams("parallel", "arbitrary"),
        name="gdn",
    )(gx, z, ab, alog, dtb, gnorm)


def _memkv_kernel(m_ref, gain_ref, w_ref, k_ref, v_ref, *, d):
    mn = _rms(m_ref[...], gain_ref[0]).astype(BF16)
    k_ref[0] = _dot(mn, w_ref[0, :, :d]).astype(BF16)
    v_ref[0] = _dot(mn, w_ref[0, :, d:]).astype(BF16)


def _memkv(mem2d, gains, w_kv):
    R, D = mem2d.shape
    L = w_kv.shape[0]
    return pl.pallas_call(
        functools.partial(_memkv_kernel, d=D),
        out_shape=[jax.ShapeDtypeStruct((L, R, D), BF16), jax.ShapeDtypeStruct((L, R, D), BF16)],
        grid=(L,),
        in_specs=[
            pl.BlockSpec((R, D), lambda l: (0, 0)),
            pl.BlockSpec((1, 1, D), lambda l: (l, 0, 0)),
            pl.BlockSpec((1, D, 2 * D), lambda l: (l, 0, 0)),
        ],
        out_specs=[
            pl.BlockSpec((1, R, D), lambda l: (l, 0, 0)),
            pl.BlockSpec((1, R, D), lambda l: (l, 0, 0)),
        ],
        compiler_params=_cparams("parallel"),
        name="memkv",
    )(mem2d, gains, w_kv)


def _mixout_xattn_kernel(h_ref, sb_ref, go_ref, wmix_ref, gain_ref, wq_ref, k_ref, v_ref, wo_ref, o_ref,
                        *, nheads, nsub, sbw):
    Dh = XATTN_HEAD_DIM
    ts = h_ref.shape[0] // nsub
    R = range(nsub)
    H = range(nheads)
    rows = [slice(s * ts, (s + 1) * ts) for s in R]
    ls = [slice(hd * Dh, (hd + 1) * Dh) for hd in H]
    x = [h_ref[rows[s], :] + (_dot(sb_ref[rows[s], :], wmix_ref[:sbw, :]) + _dot(go_ref[rows[s], :], wmix_ref[sbw:, :]))
         for s in R]
    q = [_dot(_rms(x[s], gain_ref[...]).astype(BF16), wq_ref[...]).astype(BF16) for s in R]
    sc = [[_dot_nt(q[s][:, ls[hd]], k_ref[:, ls[hd]]) for hd in H] for s in R]
    p = [[jnp.exp(sc[s][hd] - jnp.max(sc[s][hd], axis=-1, keepdims=True)) for hd in H] for s in R]
    pv = [[_dot(p[s][hd].astype(BF16), v_ref[:, ls[hd]]) for hd in H] for s in R]
    for s in R:
        o = jnp.concatenate([(pv[s][hd] / jnp.sum(p[s][hd], axis=-1, keepdims=True)).astype(BF16) for hd in H],
                            axis=-1)
        o_ref[rows[s], :] = x[s] + _dot(o, wo_ref[...])


def _mixout_xattn(h, sb, go, wmix, gain, wq, kmem, vmem, wo, layer, *, batch, tm=512, nsub=2):
    T, D = h.shape
    S = T // batch
    per = S // tm
    M = kmem.shape[1] // batch
    sbw = sb.shape[1]
    gw = go.shape[1]
    row = lambda i: (i, 0)
    fixed = lambda i: (0, 0)
    weight = lambda i: (layer, 0, 0)
    mem_rows = lambda i: (layer, i // per, 0)
    return pl.pallas_call(
        functools.partial(_mixout_xattn_kernel, nheads=D // XATTN_HEAD_DIM, nsub=nsub, sbw=sbw),
        out_shape=jax.ShapeDtypeStruct((T, D), F32),
        grid=(T // tm,),
        in_specs=[
            pl.BlockSpec((tm, D), row),
            pl.BlockSpec((tm, sbw), row),
            pl.BlockSpec((tm, gw), row),
            pl.BlockSpec((None, sbw + gw, D), weight),
            pl.BlockSpec((1, D), fixed),
            pl.BlockSpec((None, D, D), weight),
            pl.BlockSpec((None, M, D), mem_rows),
            pl.BlockSpec((None, M, D), mem_rows),
            pl.BlockSpec((None, D, D), weight),
        ],
        out_specs=pl.BlockSpec((tm, D), row),
        compiler_params=_cparams("parallel"),
        name="mixout_xattn",
    )(h, sb, go, wmix, gain, wq, kmem, vmem, wo)


def _lane_vec(v):
    return jnp.zeros((1, LANES), F32).at[0, :v.shape[0]].set(v.astype(F32))


def kernel(x, mem, ffn1_norm, ffn1_w_in, ffn1_w_out, mix_norm, w_in, conv_w, a_log, dt_bias, sb_out_norm, gdn_out_norm, w_out, xattn_norm, mem_norm, xattn_w_q, xattn_w_kv, xattn_w_o, ffn2_norm, ffn2_w_in, ffn2_w_out, final_norm):
    B, S, D = x.shape
    L = ffn1_norm.shape[0]
    nh_gdn = a_log.shape[1]
    gw = nh_gdn * GDN_HEAD_DIM
    sbw = sb_out_norm.shape[1]
    M = mem.shape[1]
    assert S % 512 == 0 and (B * S) % 1024 == 0 and D % LANES == 0 and sbw % LANES == 0
    assert w_in.shape[2] == 3 * sbw + 4 * gw + 2 * nh_gdn and 2 * nh_gdn <= LANES
    T = B * S

    bf = lambda w: w.astype(BF16)
    ffn1_wi, ffn1_wo, ffn2_wi, ffn2_wo = bf(ffn1_w_in), bf(ffn1_w_out), bf(ffn2_w_in), bf(ffn2_w_out)
    w_mix = bf(jnp.pad(w_in, ((0, 0), (0, 0), (0, LANES - 2 * nh_gdn))))
    w_mixout, w_xo = bf(w_out), bf(xattn_w_o)
    conv_w8 = jnp.pad(conv_w.astype(F32), ((0, 0), (0, 8 - CONV_WIDTH), (0, 0)))
    wq = bf(xattn_w_q * (XATTN_HEAD_DIM ** -0.5))
    kmem, vmem = _memkv(mem.reshape(B * M, D), mem_norm.reshape(L, 1, D), bf(xattn_w_kv))

    h = x.reshape(T, D)
    for l in range(L):
        h = _ffn(h, ffn1_norm[l][None], ffn1_wi, ffn1_wo, l)
        q, k, v, gx, z, ab = _inproj(h, mix_norm[l][None], w_mix, l, conv_w8[l], sbw=sbw, gw=gw, batch=B)
        sb = _sb_attention(q, k, v, sb_out_norm[l][None], batch=B)
        go = _gdn(gx, z, ab, _lane_vec(a_log[l]), _lane_vec(dt_bias[l]),
                  gdn_out_norm[l][None], batch=B, nheads=nh_gdn)
        h = _mixout_xattn(h, sb, go, w_mixout, xattn_norm[l][None], wq, kmem, vmem, w_xo, l, batch=B)
        h = _ffn(h, ffn2_norm[l][None], ffn2_wi, ffn2_wo, l, final_norm[None] if l == L - 1 else None)
    return h.reshape(B, S, D)
```
